```python
import math
import jax, jax.numpy as jnp
from jax import lax
import numpy as np

D_MODEL = 1024
BATCH = 8
SEQ = 4096
DEPTH = 1

CONV_CH = 512
CONV_WIDTH = 31
ATTN_HEADS = 8
HEAD_DIM = 64
ATTN_W = ATTN_HEADS * HEAD_DIM
MOBA_BLOCK = 256
MOBA_TOPK = 3
Q_CHUNK = 128
PEER_HEADS = 8
PEER_NKEYS = 128
PEER_EXPERTS = PEER_NKEYS * PEER_NKEYS
PEER_QDIM = 256
PEER_TOPK = 16
PEER_TOK_CHUNK = 128
NORM_EPS = 1e-6
IN_SPLITS = (2 * CONV_CH, 2 * CONV_CH + ATTN_W, 2 * CONV_CH + 2 * ATTN_W,
             2 * CONV_CH + 3 * ATTN_W, 2 * CONV_CH + 3 * ATTN_W + D_MODEL)
IN_COLS = 2 * CONV_CH + 3 * ATTN_W + 2 * D_MODEL

kernel_name = 'hybrid_conformer_moba_peer_block'


def rms_norm(x, g):
    xf = x.astype(jnp.float32)
    y = xf * lax.rsqrt(jnp.mean(xf * xf, axis=-1, keepdims=True) + NORM_EPS)
    return (y * g.astype(jnp.float32)).astype(x.dtype)


def layer_norm(x, g, b):
    xf = x.astype(jnp.float32)
    mu = jnp.mean(xf, axis=-1, keepdims=True)
    xc = xf - mu
    var = jnp.mean(xc * xc, axis=-1, keepdims=True)
    y = xc * lax.rsqrt(var + NORM_EPS) * g.astype(jnp.float32) + b.astype(jnp.float32)
    return y.astype(x.dtype)


def alibi_slopes(n_heads):
    return jnp.asarray([2.0 ** (-8.0 * (i + 1) / n_heads) for i in range(n_heads)], dtype=jnp.float32)


def conformer_conv(u, conv_w, conv_b, ln_g, ln_b, w_pw, b_pw):
    a, gte = jnp.split(u, 2, axis=-1)
    z = a * jax.nn.sigmoid(gte)
    z = jnp.pad(z, ((0, 0), (CONV_WIDTH - 1, 0), (0, 0)))
    z = lax.conv_general_dilated(z, conv_w[:, None, :].astype(z.dtype), window_strides=(1,),
                                 padding='VALID', dimension_numbers=('NWC', 'WIO', 'NWC'),
                                 feature_group_count=CONV_CH) + conv_b
    z = jax.nn.silu(layer_norm(z, ln_g, ln_b))
    return z @ w_pw + b_pw


def moba_attention(q, k, v):
    B, H, S, dh = q.shape
    nb = -(-S // MOBA_BLOCK)
    s_pad = nb * MOBA_BLOCK
    pad = ((0, 0), (0, 0), (0, s_pad - S), (0, 0))
    kb = jnp.pad(k, pad).reshape(B, H, nb, MOBA_BLOCK, dh)
    vb = jnp.pad(v, pad).reshape(B, H, nb, MOBA_BLOCK, dh)
    kmean = jnp.mean(kb.astype(jnp.float32), axis=3)
    n_sel = min(MOBA_TOPK, nb)
    nc = S // Q_CHUNK
    slopes = alibi_slopes(H)
    scale = dh ** -0.5
    offs = jnp.arange(MOBA_BLOCK)
    qloc = jnp.arange(Q_CHUNK)
    hidx = jnp.arange(H)[:, None, None]

    def chunk(n):
        b = n // nc
        c = n % nc
        t0 = c * Q_CHUNK
        blk = t0 // MOBA_BLOCK
        qc = lax.dynamic_slice_in_dim(lax.dynamic_index_in_dim(q, b, 0, keepdims=False), t0, Q_CHUNK, axis=1)
        qf = qc.astype(jnp.float32) * scale
        kb_b = lax.dynamic_index_in_dim(kb, b, 0, keepdims=False)
        vb_b = lax.dynamic_index_in_dim(vb, b, 0, keepdims=False)
        km_b = lax.dynamic_index_in_dim(kmean, b, 0, keepdims=False)
        tpos = t0 + qloc
        bscore = jnp.einsum('hqd,hnd->hqn', qf, km_b)
        bscore = jnp.where(jnp.arange(nb) < blk, bscore, -jnp.inf)
        _, sel = lax.top_k(bscore, n_sel)
        sel_ok = sel < blk
        ksel = kb_b[hidx, sel]
        vsel = vb_b[hidx, sel]
        spos = sel[..., None] * MOBA_BLOCK + offs
        dist = (tpos[None, :, None, None] - spos).astype(jnp.float32)
        lsel = jnp.einsum('hqd,hqnkd->hqnk', qf, ksel.astype(jnp.float32)) - slopes[:, None, None, None] * dist
        lsel = jnp.where(sel_ok[..., None], lsel, -jnp.inf).reshape(H, Q_CHUNK, n_sel * MOBA_BLOCK)
        kown = lax.dynamic_index_in_dim(kb_b, blk, 1, keepdims=False)
        vown = lax.dynamic_index_in_dim(vb_b, blk, 1, keepdims=False)
        opos = blk * MOBA_BLOCK + offs
        odist = (tpos[:, None] - opos[None, :]).astype(jnp.float32)
        lown = jnp.einsum('hqd,hkd->hqk', qf, kown.astype(jnp.float32)) - slopes[:, None, None] * odist
        lown = jnp.where((opos[None, :] <= tpos[:, None])[None], lown, -jnp.inf)
        p = jax.nn.softmax(jnp.concatenate([lsel, lown], axis=-1), axis=-1)
        p_sel = p[..., :n_sel * MOBA_BLOCK]
        p_own = p[..., n_sel * MOBA_BLOCK:]
        out = (jnp.einsum('hqk,hqkd->hqd', p_sel, vsel.reshape(H, Q_CHUNK, n_sel * MOBA_BLOCK, dh).astype(jnp.float32))
               + jnp.einsum('hqk,hkd->hqd', p_own, vown.astype(jnp.float32)))
        return out.astype(q.dtype)

    o = lax.map(chunk, jnp.arange(B * nc))
    o = o.reshape(B, nc, H, Q_CHUNK, dh).transpose(0, 1, 3, 2, 4)
    return o.reshape(B, S, H * dh)


def peer_ffn(h, w_q, keys1, keys2, u_tab, v_tab):
    B, S, D = h.shape
    T = B * S
    ht = h.reshape(T, D)
    q = (ht @ w_q).reshape(T, PEER_HEADS, 2, PEER_QDIM // 2).astype(jnp.float32)
    s1 = jnp.einsum('thc,nc->thn', q[:, :, 0], keys1.astype(jnp.float32))
    s2 = jnp.einsum('thc,nc->thn', q[:, :, 1], keys2.astype(jnp.float32))
    v1, i1 = lax.top_k(s1, PEER_TOPK)
    v2, i2 = lax.top_k(s2, PEER_TOPK)
    cand = (v1[..., :, None] + v2[..., None, :]).reshape(T, PEER_HEADS, PEER_TOPK * PEER_TOPK)
    cidx = (i1[..., :, None] * PEER_NKEYS + i2[..., None, :]).reshape(T, PEER_HEADS, PEER_TOPK * PEER_TOPK)
    sc, pos = lax.top_k(cand, PEER_TOPK)
    eidx = jnp.take_along_axis(cidx, pos, axis=-1)
    g = jax.nn.softmax(sc, axis=-1)
    n_ch = T // PEER_TOK_CHUNK
    E = PEER_HEADS * PEER_TOPK
    xs = (ht.reshape(n_ch, PEER_TOK_CHUNK, D),
          eidx.reshape(n_ch, PEER_TOK_CHUNK, E),
          g.reshape(n_ch, PEER_TOK_CHUNK, E))

    def body(args):
        xc, ec, gc = args
        uc = u_tab[ec]
        a = jax.nn.gelu(jnp.einsum('td,ted->te', xc, uc, preferred_element_type=jnp.float32), approximate=False)
        vc = v_tab[ec]
        y = jnp.einsum('te,ted->td', (gc * a).astype(vc.dtype), vc, preferred_element_type=jnp.float32)
        return y.astype(h.dtype)

    return lax.map(body, xs).reshape(B, S, D)


def setup_inputs(seed: int = 0) -> dict:
    key = jax.random.key(seed)
    ks = jax.random.split(key, 20)
    f32 = jnp.float32
    L = DEPTH
    nrm = lambda k, shape, s: jax.random.normal(k, shape, f32) * s
    return {
        'x': jax.random.normal(ks[0], (BATCH, SEQ, D_MODEL), f32),
        'g_norm1': 1.0 + nrm(ks[1], (L, D_MODEL), 0.02),
        'w_in': nrm(ks[2], (L, D_MODEL, IN_COLS), D_MODEL ** -0.5),
        'conv_w': nrm(ks[3], (L, CONV_WIDTH, CONV_CH), CONV_WIDTH ** -0.5),
        'conv_b': nrm(ks[4], (L, CONV_CH), 0.02),
        'conv_ln_g': 1.0 + nrm(ks[5], (L, CONV_CH), 0.02),
        'conv_ln_b': nrm(ks[6], (L, CONV_CH), 0.02),
        'w_conv_pw': nrm(ks[7], (L, CONV_CH, D_MODEL), CONV_CH ** -0.5),
        'b_conv_pw': nrm(ks[8], (L, D_MODEL), 0.02),
        'w_attn_out': nrm(ks[9], (L, ATTN_W, D_MODEL), ATTN_W ** -0.5),
        'w_out': nrm(ks[10], (L, D_MODEL, D_MODEL), D_MODEL ** -0.5),
        'g_norm2': 1.0 + nrm(ks[11], (L, D_MODEL), 0.02),
        'w_peer_q': nrm(ks[12], (L, D_MODEL, PEER_HEADS * PEER_QDIM), D_MODEL ** -0.5),
        'peer_keys1': nrm(ks[13], (L, PEER_NKEYS, PEER_QDIM // 2), (PEER_QDIM // 2) ** -0.5),
        'peer_keys2': nrm(ks[14], (L, PEER_NKEYS, PEER_QDIM // 2), (PEER_QDIM // 2) ** -0.5),
        'peer_u': nrm(ks[15], (L, PEER_EXPERTS, D_MODEL), D_MODEL ** -0.5),
        'peer_v': nrm(ks[16], (L, PEER_EXPERTS, D_MODEL), 0.25),
        'g_final': 1.0 + nrm(ks[17], (D_MODEL,), 0.02),
    }


def reference(x, g_norm1, w_in, conv_w, conv_b, conv_ln_g, conv_ln_b, w_conv_pw, b_conv_pw,
              w_attn_out, w_out, g_norm2, w_peer_q, peer_keys1, peer_keys2, peer_u, peer_v, g_final):
    B, S, D = x.shape
    for l in range(DEPTH):
        h = rms_norm(x, g_norm1[l])
        z = h @ w_in[l]
        u_conv, q, k, v, gate_c, gate_a = jnp.split(z, IN_SPLITS, axis=-1)
        y_conv = conformer_conv(u_conv, conv_w[l], conv_b[l], conv_ln_g[l], conv_ln_b[l], w_conv_pw[l], b_conv_pw[l])
        to_heads = lambda t: t.reshape(B, S, ATTN_HEADS, HEAD_DIM).transpose(0, 2, 1, 3)
        y_attn = moba_attention(to_heads(q), to_heads(k), to_heads(v)) @ w_attn_out[l]
        mix = jax.nn.sigmoid(gate_c) * y_conv + jax.nn.sigmoid(gate_a) * y_attn
        x = x + mix @ w_out[l]
        x = x + peer_ffn(rms_norm(x, g_norm2[l]), w_peer_q[l], peer_keys1[l], peer_keys2[l], peer_u[l], peer_v[l])
    return rms_norm(x, g_final)
```

```python
import functools
import math

import jax
import jax.numpy as jnp
from jax import lax
from jax.experimental import pallas as pl
from jax.experimental.pallas import tpu as pltpu

F32 = jnp.float32
BF16 = jnp.bfloat16

CONV_CH = 512
CONV_WIDTH = 31
ATTN_HEADS = 8
HEAD_DIM = 64
ATTN_W = ATTN_HEADS * HEAD_DIM
MOBA_BLOCK = 256
MOBA_TOPK = 3
PEER_HEADS = 8
PEER_NKEYS = 128
PEER_QDIM = 256
PEER_TOPK = 16
NORM_EPS = 1e-6

LANES = 128
CONV_HALO = 32
NOT_RANKED = 99.0
VMEM_LIMIT = 56 * 1024 * 1024

NEG_INF = float("-inf")


def _params(sem):
    return pltpu.CompilerParams(dimension_semantics=sem, vmem_limit_bytes=VMEM_LIMIT)


def _nt_dot(a, b):
    return lax.dot_general(a, b, (((1,), (1,)), ((), ())), preferred_element_type=F32)


def _split_bf16(x):
    hi = x.astype(BF16)
    lo = (x - hi.astype(F32)).astype(BF16)
    return hi, lo


def _inproj_kernel(x_ref, g_ref, w_ref, u_ref, q_ref, k_ref, v_ref, gc_ref, ga_ref, km_ref, *, d_model):
    x = x_ref[...]
    h = x * lax.rsqrt(jnp.mean(x * x, axis=-1, keepdims=True) + NORM_EPS) * g_ref[...]
    hb = h.astype(BF16)

    def proj(a, b):
        return jnp.dot(hb, w_ref[:, a:b], preferred_element_type=F32)

    c0 = 2 * CONV_CH
    u_ref[...] = proj(0, c0).astype(BF16)
    q_ref[...] = (proj(c0, c0 + ATTN_W) * (HEAD_DIM ** -0.5)).astype(BF16)
    kf = proj(c0 + ATTN_W, c0 + 2 * ATTN_W)
    k_ref[...] = kf.astype(BF16)
    v_ref[...] = proj(c0 + 2 * ATTN_W, c0 + 3 * ATTN_W).astype(BF16)
    gc_ref[...] = proj(c0 + 3 * ATTN_W, c0 + 3 * ATTN_W + d_model).astype(BF16)
    ga_ref[...] = proj(c0 + 3 * ATTN_W + d_model, c0 + 3 * ATTN_W + 2 * d_model).astype(BF16)
    for blk in range(kf.shape[0] // MOBA_BLOCK):
        km_ref[blk] = jnp.mean(kf[blk * MOBA_BLOCK:(blk + 1) * MOBA_BLOCK], axis=0, keepdims=True)


def _inproj(x2, g1, w_in_b, tm):
    t, d = x2.shape
    n_in = w_in_b.shape[1]
    tok = lambda w: pl.BlockSpec((tm, w), lambda i: (i, 0))
    const = lambda shape: pl.BlockSpec(shape, lambda i: (0,) * len(shape))
    nbt = tm // MOBA_BLOCK
    return pl.pallas_call(
        functools.partial(_inproj_kernel, d_model=d),
        grid=(t // tm,),
        in_specs=[tok(d), const((1, d)), const((d, n_in))],
        out_specs=[tok(2 * CONV_CH), tok(ATTN_W), tok(ATTN_W), tok(ATTN_W), tok(d), tok(d),
                   pl.BlockSpec((nbt, 1, ATTN_W), lambda i: (i, 0, 0))],
        out_shape=[jax.ShapeDtypeStruct((t, 2 * CONV_CH), BF16),
                   jax.ShapeDtypeStruct((t, ATTN_W), BF16),
                   jax.ShapeDtypeStruct((t, ATTN_W), BF16),
                   jax.ShapeDtypeStruct((t, ATTN_W), BF16),
                   jax.ShapeDtypeStruct((t, d), BF16),
                   jax.ShapeDtypeStruct((t, d), BF16),
                   jax.ShapeDtypeStruct((t // MOBA_BLOCK, 1, ATTN_W), F32)],
        compiler_params=_params(("parallel",)),
    )(x2, g1, w_in_b)


def _conv_kernel(u_ref, halo_ref, gc_ref, cw_ref, cb_ref, lg_ref, lb_ref, wpw_ref, bpw_ref, o_ref, zext_ref, *, ts):
    def glu(u):
        u = u.astype(F32)
        return u[:, :CONV_CH] * jax.nn.sigmoid(u[:, CONV_CH:])

    zh = glu(halo_ref[0])
    zh = jnp.where(pl.program_id(1) == 0, jnp.zeros_like(zh), zh)
    zext_ref[0:CONV_HALO, :] = zh
    zext_ref[CONV_HALO:CONV_HALO + ts, :] = glu(u_ref[0])
    acc = jnp.zeros((ts, CONV_CH), F32) + cb_ref[...]
    base = CONV_HALO - (CONV_WIDTH - 1)
    for w in range(CONV_WIDTH):
        acc = acc + zext_ref[base + w:base + w + ts, :] * cw_ref[w:w + 1, :]
    mu = jnp.mean(acc, axis=-1, keepdims=True)
    xc = acc - mu
    var = jnp.mean(xc * xc, axis=-1, keepdims=True)
    y = xc * lax.rsqrt(var + NORM_EPS) * lg_ref[...] + lb_ref[...]
    y = y * jax.nn.sigmoid(y)
    yc = jnp.dot(y.astype(BF16), wpw_ref[...], preferred_element_type=F32) + bpw_ref[...]
    o_ref[0] = (jax.nn.sigmoid(gc_ref[0].astype(F32)) * yc).astype(BF16)


def _conv(u3, gc3, conv_w, conv_b, ln_g, ln_b, wpw_b, bpw, ts):
    b, s, d2 = u3.shape
    d = gc3.shape[-1]
    hpb = ts // CONV_HALO
    const = lambda shape: pl.BlockSpec(shape, lambda bi, si: (0,) * len(shape))
    return pl.pallas_call(
        functools.partial(_conv_kernel, ts=ts),
        grid=(b, s // ts),
        in_specs=[pl.BlockSpec((1, ts, d2), lambda bi, si: (bi, si, 0)),
                  pl.BlockSpec((1, CONV_HALO, d2), lambda bi, si: (bi, jnp.maximum(si * hpb - 1, 0), 0)),
                  pl.BlockSpec((1, ts, d), lambda bi, si: (bi, si, 0)),
                  const((CONV_WIDTH, CONV_CH)), const((1, CONV_CH)), const((1, CONV_CH)), const((1, CONV_CH)),
                  const((CONV_CH, d)), const((1, d))],
        out_specs=pl.BlockSpec((1, ts, d), lambda bi, si: (bi, si, 0)),
        out_shape=jax.ShapeDtypeStruct((b, s, d), BF16),
        scratch_shapes=[pltpu.VMEM((CONV_HALO + ts, CONV_CH), F32)],
        compiler_params=_params(("parallel", "parallel")),
    )(u3, u3, gc3, conv_w, conv_b, ln_g, ln_b, wpw_b, bpw)


def _attn_kernel(slopes_ref, q_ref, k_ref, v_ref, km_ref, o_ref, *, nb):
    blk = MOBA_BLOCK
    hp = pl.program_id(1)
    qb = pl.program_id(2)
    q2 = q_ref[0]
    lane = lax.broadcasted_iota(jnp.int32, (blk, LANES), 1)
    ri = lax.broadcasted_iota(jnp.int32, (blk, blk), 0)
    ci = lax.broadcasted_iota(jnp.int32, (blk, blk), 1)
    dloc = (ri - ci).astype(F32)
    biota = lax.broadcasted_iota(jnp.int32, (blk, nb), 1)
    kmh, kml = _split_bf16(km_ref[0])
    k_own = k_ref[0, pl.ds(pl.multiple_of(qb * blk, blk), blk), :]
    v_own = v_ref[0, pl.ds(pl.multiple_of(qb * blk, blk), blk), :]
    heads_per_group = LANES // HEAD_DIM
    outs = []
    for h2 in range(heads_per_group):
        slope = slopes_ref[hp * heads_per_group + h2]
        hmask = (lane >= HEAD_DIM * h2) & (lane < HEAD_DIM * (h2 + 1))
        qh = jnp.where(hmask, q2, jnp.zeros_like(q2))

        bs = _nt_dot(qh, kmh) + _nt_dot(qh, kml)
        bs = jnp.where(biota < qb, bs, NEG_INF)
        sel = jnp.zeros((blk, nb), F32)
        for _ in range(MOBA_TOPK):
            m = jnp.max(bs, axis=1, keepdims=True)
            idx = jnp.min(jnp.where(bs == m, biota, nb), axis=1, keepdims=True)
            hit = biota == idx
            sel = jnp.where(hit & (m > NEG_INF), 1.0, sel)
            bs = jnp.where(hit, NEG_INF, bs)
        selb = sel.astype(BF16)

        s = _nt_dot(qh, k_own) - slope * dloc
        s = jnp.where(ci <= ri, s, NEG_INF)
        m0 = jnp.max(s, axis=1, keepdims=True)
        p = jnp.exp(s - m0)
        l0 = jnp.sum(p, axis=1, keepdims=True)
        acc0 = jnp.dot(p.astype(BF16), v_own, preferred_element_type=F32)

        def body(j, carry, qh=qh, slope=slope, selb=selb):
            m, l, acc = carry
            off = pl.multiple_of(j * blk, blk)
            kj = k_ref[0, pl.ds(off, blk), :]
            vj = v_ref[0, pl.ds(off, blk), :]
            dist = dloc + ((qb - j) * blk).astype(F32)
            s = _nt_dot(qh, kj) - slope * dist
            expand = (lax.broadcasted_iota(jnp.int32, (nb, blk), 0) == j).astype(F32).astype(BF16)
            selx = jnp.dot(selb, expand, preferred_element_type=F32)
            s = jnp.where(selx > 0.5, s, NEG_INF)
            m_new = jnp.maximum(m, jnp.max(s, axis=1, keepdims=True))
            alpha = jnp.exp(m - m_new)
            p = jnp.exp(s - m_new)
            l = alpha * l + jnp.sum(p, axis=1, keepdims=True)
            acc = alpha * acc + jnp.dot(p.astype(BF16), vj, preferred_element_type=F32)
            return m_new, l, acc

        _, l, acc = lax.fori_loop(0, qb, body, (m0, l0, acc0))
        outs.append(acc / l)
    o = outs[0]
    for h2 in range(1, heads_per_group):
        o = jnp.where(lane >= HEAD_DIM * h2, outs[h2], o)
    o_ref[0] = o.astype(BF16)


def _attn(q3, k3, v3, km3, slopes):
    b, s, w = q3.shape
    nb = s // MOBA_BLOCK
    return pl.pallas_call(
        functools.partial(_attn_kernel, nb=nb),
        grid=(b, w // LANES, nb),
        in_specs=[pl.BlockSpec(memory_space=pltpu.SMEM),
                  pl.BlockSpec((1, MOBA_BLOCK, LANES), lambda bi, hp, qb: (bi, qb, hp)),
                  pl.BlockSpec((1, s, LANES), lambda bi, hp, qb: (bi, 0, hp)),
                  pl.BlockSpec((1, s, LANES), lambda bi, hp, qb: (bi, 0, hp)),
                  pl.BlockSpec((1, nb, LANES), lambda bi, hp, qb: (bi, 0, hp))],
        out_specs=pl.BlockSpec((1, MOBA_BLOCK, LANES), lambda bi, hp, qb: (bi, qb, hp)),
        out_shape=jax.ShapeDtypeStruct((b, s, w), BF16),
        compiler_params=_params(("parallel", "parallel", "arbitrary")),
    )(slopes, q3, k3, v3, km3)


def _merge_kernel(at_ref, cm_ref, ga_ref, x_ref, wao_ref, wo_ref, g2_ref, wq_ref, x1_ref, h2_ref, qp_ref):
    ya = jnp.dot(at_ref[...], wao_ref[...], preferred_element_type=F32)
    mix = cm_ref[...].astype(F32) + jax.nn.sigmoid(ga_ref[...].astype(F32)) * ya
    x1 = x_ref[...] + jnp.dot(mix.astype(BF16), wo_ref[...], preferred_element_type=F32)
    x1_ref[...] = x1
    h2 = x1 * lax.rsqrt(jnp.mean(x1 * x1, axis=-1, keepdims=True) + NORM_EPS) * g2_ref[...]
    h2b = h2.astype(BF16)
    h2_ref[...] = h2b
    qp_ref[...] = jnp.dot(h2b, wq_ref[...], preferred_element_type=F32)


def _merge(at2, cm2, ga2, x2, wao_b, wo_b, g2, wq_b, tm):
    t, d = x2.shape
    nq = wq_b.shape[1]
    tok = lambda w: pl.BlockSpec((tm, w), lambda i: (i, 0))
    const = lambda shape: pl.BlockSpec(shape, lambda i: (0,) * len(shape))
    return pl.pallas_call(
        _merge_kernel,
        grid=(t // tm,),
        in_specs=[tok(ATTN_W), tok(d), tok(d), tok(d),
                  const((ATTN_W, d)), const((d, d)), const((1, d)), const((d, nq))],
        out_specs=[tok(d), tok(d), tok(nq)],
        out_shape=[jax.ShapeDtypeStruct((t, d), F32),
                   jax.ShapeDtypeStruct((t, d), BF16),
                   jax.ShapeDtypeStruct((t, nq), F32)],
        compiler_params=_params(("parallel",)),
    )(at2, cm2, ga2, x2, wao_b, wo_b, g2, wq_b)


def _top_ranked(x, kiota):
    rank = jnp.full(x.shape, NOT_RANKED, F32)
    vals = []
    for r in range(PEER_TOPK):
        m = jnp.max(x, axis=0, keepdims=True)
        idx = jnp.min(jnp.where(x == m, kiota, float(x.shape[0])), axis=0, keepdims=True)
        hit = kiota == idx
        x = jnp.where(hit, NEG_INF, x)
        rank = jnp.where(hit, float(r), rank)
        vals.append(m)
    return rank, vals


def _peer_topk_kernel(qp_ref, k1_ref, k2_ref, r2_ref, e2_ref, l1_ref, c1_ref):
    half = PEER_QDIM // 2
    tt = qp_ref.shape[0]
    qp = qp_ref[...]
    kiota = lax.broadcasted_iota(jnp.int32, (PEER_NKEYS, tt), 0).astype(F32)

    def scores(keys, qside):
        kh, kl = _split_bf16(keys)
        qh, ql = _split_bf16(qside)
        return _nt_dot(kh, qh) + (_nt_dot(kh, ql) + _nt_dot(kl, qh))

    s1 = scores(k1_ref[...], qp[:, :half])
    s2 = scores(k2_ref[...], qp[:, half:])
    rank1, v1 = _top_ranked(s1, kiota)
    rank2, v2 = _top_ranked(s2, kiota)

    k = PEER_TOPK
    g = 8
    sub = lax.broadcasted_iota(jnp.int32, (g, tt), 0).astype(F32)
    v2a = jnp.concatenate(v2[:g], axis=0)
    v2b = jnp.concatenate(v2[g:], axis=0)
    v1b = jnp.concatenate(v1[g:], axis=0)
    cands = [v1[0] + v2a, v1[0] + v2b] + [v1[r] + v2a for r in range(1, g)] + [v1b + v2[0]]
    poss = [sub, sub + float(g)] + [sub + float(r * k) for r in range(1, g)] + [(sub + float(g)) * float(k)]
    orig = list(cands)
    picked = [jnp.zeros((g, tt), F32) for _ in cands]
    for _ in range(k):
        m = functools.reduce(jnp.maximum, cands)
        m = jnp.max(m, axis=0, keepdims=True)
        pm = functools.reduce(jnp.minimum, [jnp.where(c == m, p, 1e9) for c, p in zip(cands, poss)])
        pm = jnp.min(pm, axis=0, keepdims=True)
        hits = [p == pm for p in poss]
        cands = [jnp.where(h, NEG_INF, c) for h, c in zip(hits, cands)]
        picked = [jnp.where(h, 1.0, s) for h, s in zip(hits, picked)]
    cmax = v1[0] + v2[0]
    z = functools.reduce(
        lambda a, b_: a + b_,
        [jnp.sum(jnp.where(s > 0.5, jnp.exp(c - cmax), 0.0), axis=0, keepdims=True) for s, c in zip(picked, orig)])
    lrow = [jnp.sum(picked[0] + picked[1], axis=0, keepdims=True)]
    lrow += [jnp.sum(picked[r + 1], axis=0, keepdims=True) for r in range(1, g)]
    lvec = jnp.concatenate(lrow + [picked[g + 1]], axis=0)

    l1 = jnp.zeros((PEER_NKEYS, tt), F32)
    for r in range(k):
        l1 = jnp.where(rank1 == float(r), lvec[r:r + 1, :], l1)
    r2_ref[0] = rank2
    e2_ref[0] = jnp.exp(s2 - v2[0])
    l1_ref[0] = l1
    c1_ref[0] = jnp.exp(s1 - v1[0]) / z


def _peer_topk(qp, keys1, keys2, tt):
    t = qp.shape[0]
    out = jax.ShapeDtypeStruct((PEER_HEADS, PEER_NKEYS, t), F32)
    ospec = pl.BlockSpec((1, PEER_NKEYS, tt), lambda i, h: (h, 0, i))
    kspec = pl.BlockSpec(keys1.shape, lambda i, h: (0, 0))
    return pl.pallas_call(
        _peer_topk_kernel,
        grid=(t // tt, PEER_HEADS),
        in_specs=[pl.BlockSpec((tt, PEER_QDIM), lambda i, h: (i, h)), kspec, kspec],
        out_specs=[ospec] * 4,
        out_shape=[out] * 4,
        compiler_params=_params(("parallel", "parallel")),
    )(qp, keys1, keys2)


def _peer_dense_kernel(h2_ref, u_ref, vt_ref, r2_ref, e2_ref, l1_ref, c1_ref, x1_ref, gf_ref, o_ref, acc_ref, *, n1_per_tile):
    j = pl.program_id(1)

    @pl.when(j == 0)
    def _():
        acc_ref[...] = jnp.zeros_like(acc_ref)

    at = _nt_dot(u_ref[...], h2_ref[...])
    parts = []
    for a in range(n1_per_tile):
        slab = at[a * PEER_NKEYS:(a + 1) * PEER_NKEYS, :]
        gate = None
        for h in range(PEER_HEADS):
            term = jnp.where(r2_ref[h] < l1_ref[h, a:a + 1, :], e2_ref[h], 0.0) * c1_ref[h, a:a + 1, :]
            gate = term if gate is None else gate + term
        act = 0.5 * slab * (1.0 + lax.erf(slab * math.sqrt(0.5)))
        parts.append((gate * act).astype(BF16))
    ga = jnp.concatenate(parts, axis=0)
    acc_ref[...] += jnp.dot(vt_ref[...], ga, preferred_element_type=F32)

    @pl.when(j == pl.num_programs(1) - 1)
    def _():
        x2 = x1_ref[...] + acc_ref[...].T
        o_ref[...] = x2 * lax.rsqrt(jnp.mean(x2 * x2, axis=-1, keepdims=True) + NORM_EPS) * gf_ref[...]


def _peer_dense(h2, u_b, vt_b, r2, e2, l1, c1, x1, gf, tt, w):
    t, d = h2.shape
    ne = u_b.shape[0]
    n1_per_tile = w // PEER_NKEYS
    key_spec = pl.BlockSpec((PEER_HEADS, PEER_NKEYS, tt), lambda i, j: (0, 0, i))
    n1_spec = pl.BlockSpec((PEER_HEADS, n1_per_tile, tt), lambda i, j: (0, j, i))
    return pl.pallas_call(
        functools.partial(_peer_dense_kernel, n1_per_tile=n1_per_tile),
        grid=(t // tt, ne // w),
        in_specs=[pl.BlockSpec((tt, d), lambda i, j: (i, 0)),
                  pl.BlockSpec((w, d), lambda i, j: (j, 0)),
                  pl.BlockSpec((d, w), lambda i, j: (0, j)),
                  key_spec, key_spec, n1_spec, n1_spec,
                  pl.BlockSpec((tt, d), lambda i, j: (i, 0)),
                  pl.BlockSpec((1, d), lambda i, j: (0, 0))],
        out_specs=pl.BlockSpec((tt, d), lambda i, j: (i, 0)),
        out_shape=jax.ShapeDtypeStruct((t, d), F32),
        scratch_shapes=[pltpu.VMEM((d, tt), F32)],
        compiler_params=_params(("parallel", "arbitrary")),
    )(h2, u_b, vt_b, r2, e2, l1, c1, x1, gf)


def _alibi_slopes(n_heads):
    return jnp.asarray([2.0 ** (-8.0 * (i + 1) / n_heads) for i in range(n_heads)], dtype=F32)


def _layer(x2, b, s, g1, w_in, conv_w, conv_b, ln_g, ln_b, w_pw, b_pw, w_ao, w_o, g2, w_q, keys1, keys2, u_tab, v_tab, gf):
    t, d = x2.shape
    row = lambda a: a.reshape(1, -1).astype(F32)
    u, q, k, v, gc, ga, km = _inproj(x2, row(g1), w_in.astype(BF16), tm=512)
    cm = _conv(u.reshape(b, s, -1), gc.reshape(b, s, d), conv_w, row(conv_b), row(ln_g), row(ln_b),
               w_pw.astype(BF16), row(b_pw), ts=512)
    at = _attn(q.reshape(b, s, ATTN_W), k.reshape(b, s, ATTN_W), v.reshape(b, s, ATTN_W),
               km.reshape(b, s // MOBA_BLOCK, ATTN_W), _alibi_slopes(ATTN_HEADS))
    x1, h2, qp = _merge(at.reshape(t, ATTN_W), cm.reshape(t, d), ga, x2, w_ao.astype(BF16), w_o.astype(BF16),
                        row(g2), w_q.astype(BF16), tm=512)
    r2, e2, l1, c1 = _peer_topk(qp, keys1, keys2, tt=256)
    return _peer_dense(h2, u_tab.astype(BF16), v_tab.astype(BF16).T, r2, e2, l1, c1, x1, gf, tt=512, w=1024)


def kernel(x, g_norm1, w_in, conv_w, conv_b, conv_ln_g, conv_ln_b, w_conv_pw, b_conv_pw, w_attn_out, w_out,
           g_norm2, w_peer_q, peer_keys1, peer_keys2, peer_u, peer_v, g_final):
    b, s, d = x.shape
    depth = w_in.shape[0]
    assert depth == 1, "the final RMSNorm is fused into the last layer's PEER kernel"
    assert s % 512 == 0 and d % LANES == 0
    x2 = x.reshape(b * s, d)
    l = 0
    out = _layer(x2, b, s, g_norm1[l], w_in[l], conv_w[l], conv_b[l], conv_ln_g[l], conv_ln_b[l], w_conv_pw[l],
                 b_conv_pw[l], w_attn_out[l], w_out[l], g_norm2[l], w_peer_q[l], peer_keys1[l], peer_keys2[l],
                 peer_u[l], peer_v[l], g_final.reshape(1, -1).astype(F32))
    return out.reshape(b, s, d)
```

```python
import functools
import math

import jax
import jax.numpy as jnp
from jax import lax
from jax.experimental import pallas as pl
from jax.experimental.pallas import tpu as pltpu

F32 = jnp.float32
BF16 = jnp.bfloat16

CONV_CH = 512
CONV_WIDTH = 31
ATTN_HEADS = 8
HEAD_DIM = 64
ATTN_W = ATTN_HEADS * HEAD_DIM
MOBA_BLOCK = 256
MOBA_TOPK = 3
PEER_HEADS = 8
PEER_NKEYS = 128
PEER_QDIM = 256
PEER_TOPK = 16
NORM_EPS = 1e-6

LANES = 128
SUBLANES = 8
CONV_HALO = 32
NOT_RANKED = 99.0
VMEM_LIMIT = 56 * 1024 * 1024

TOK_TILE = 512
TOPK_TOK_TILE = 256
PEER_TOK_TILE = 512
PEER_EXP_TILE = 1024
PEER_EXP_CHUNK = 256
ATTN_KV_GROUP = 2
ATTN_HEADS_PER_STEP = 8

NEG_INF = float("-inf")


def _params(sem):
    return pltpu.CompilerParams(dimension_semantics=sem, vmem_limit_bytes=VMEM_LIMIT)


def _nt_dot(a, b):
    return lax.dot_general(a, b, (((1,), (1,)), ((), ())), preferred_element_type=F32)


def _split_bf16(x):
    hi = x.astype(BF16)
    lo = (x - hi.astype(F32)).astype(BF16)
    return hi, lo


def _inproj_kernel(x_ref, g_ref, w_ref, u_ref, q_ref, k_ref, vt_ref, gc_ref, ga_ref, km_ref, *, d_model):
    x = x_ref[...]
    h = x * lax.rsqrt(jnp.mean(x * x, axis=-1, keepdims=True) + NORM_EPS) * g_ref[...]
    hb = h.astype(BF16)

    def proj(a, b):
        return jnp.dot(hb, w_ref[:, a:b], preferred_element_type=F32)

    c0 = 2 * CONV_CH
    u_ref[...] = proj(0, c0).astype(BF16)
    q_ref[...] = (proj(c0, c0 + ATTN_W) * (HEAD_DIM ** -0.5)).astype(BF16)
    kf = proj(c0 + ATTN_W, c0 + 2 * ATTN_W)
    k_ref[...] = kf.astype(BF16)
    vt_ref[0] = proj(c0 + 2 * ATTN_W, c0 + 3 * ATTN_W).T.astype(BF16)
    gc_ref[...] = proj(c0 + 3 * ATTN_W, c0 + 3 * ATTN_W + d_model).astype(BF16)
    ga_ref[...] = proj(c0 + 3 * ATTN_W + d_model, c0 + 3 * ATTN_W + 2 * d_model).astype(BF16)
    for blk in range(kf.shape[0] // MOBA_BLOCK):
        km_ref[blk] = jnp.mean(kf[blk * MOBA_BLOCK:(blk + 1) * MOBA_BLOCK], axis=0, keepdims=True)


def _inproj(x2, g1, w_in_b, tm, seq):
    t, d = x2.shape
    n_in = w_in_b.shape[1]
    tiles_per_seq = seq // tm
    tok = lambda w: pl.BlockSpec((tm, w), lambda i: (i, 0))
    const = lambda shape: pl.BlockSpec(shape, lambda i: (0,) * len(shape))
    nbt = tm // MOBA_BLOCK
    return pl.pallas_call(
        functools.partial(_inproj_kernel, d_model=d),
        grid=(t // tm,),
        in_specs=[tok(d), const((1, d)), const((d, n_in))],
        out_specs=[tok(2 * CONV_CH), tok(ATTN_W), tok(ATTN_W),
                   pl.BlockSpec((1, ATTN_W, tm), lambda i: (i // tiles_per_seq, 0, i % tiles_per_seq)),
                   tok(d), tok(d),
                   pl.BlockSpec((nbt, 1, ATTN_W), lambda i: (i, 0, 0))],
        out_shape=[jax.ShapeDtypeStruct((t, 2 * CONV_CH), BF16),
                   jax.ShapeDtypeStruct((t, ATTN_W), BF16),
                   jax.ShapeDtypeStruct((t, ATTN_W), BF16),
                   jax.ShapeDtypeStruct((t // seq, ATTN_W, seq), BF16),
                   jax.ShapeDtypeStruct((t, d), BF16),
                   jax.ShapeDtypeStruct((t, d), BF16),
                   jax.ShapeDtypeStruct((t // MOBA_BLOCK, 1, ATTN_W), F32)],
        compiler_params=_params(("parallel",)),
    )(x2, g1, w_in_b)


def _conv_kernel(u_ref, halo_ref, gc_ref, cw_ref, cb_ref, lg_ref, lb_ref, wpw_ref, bpw_ref, o_ref, zext_ref, *, ts):
    def glu(u):
        u = u.astype(F32)
        return u[:, :CONV_CH] * jax.nn.sigmoid(u[:, CONV_CH:])

    zh = glu(halo_ref[0])
    zh = jnp.where(pl.program_id(1) == 0, jnp.zeros_like(zh), zh)
    zext_ref[0:CONV_HALO, :] = zh
    zext_ref[CONV_HALO:CONV_HALO + ts, :] = glu(u_ref[0])
    acc = jnp.zeros((ts, CONV_CH), F32) + cb_ref[...]
    base = CONV_HALO - (CONV_WIDTH - 1)
    for w in range(CONV_WIDTH):
        acc = acc + zext_ref[base + w:base + w + ts, :] * cw_ref[w:w + 1, :]
    mu = jnp.mean(acc, axis=-1, keepdims=True)
    xc = acc - mu
    var = jnp.mean(xc * xc, axis=-1, keepdims=True)
    y = xc * lax.rsqrt(var + NORM_EPS) * lg_ref[...] + lb_ref[...]
    y = y * jax.nn.sigmoid(y)
    yc = jnp.dot(y.astype(BF16), wpw_ref[...], preferred_element_type=F32) + bpw_ref[...]
    o_ref[0] = (jax.nn.sigmoid(gc_ref[0].astype(F32)) * yc).astype(BF16)


def _conv(u3, gc3, conv_w, conv_b, ln_g, ln_b, wpw_b, bpw, ts):
    b, s, d2 = u3.shape
    d = gc3.shape[-1]
    hpb = ts // CONV_HALO
    const = lambda shape: pl.BlockSpec(shape, lambda bi, si: (0,) * len(shape))
    return pl.pallas_call(
        functools.partial(_conv_kernel, ts=ts),
        grid=(b, s // ts),
        in_specs=[pl.BlockSpec((1, ts, d2), lambda bi, si: (bi, si, 0)),
                  pl.BlockSpec((1, CONV_HALO, d2), lambda bi, si: (bi, jnp.maximum(si * hpb - 1, 0), 0)),
                  pl.BlockSpec((1, ts, d), lambda bi, si: (bi, si, 0)),
                  const((CONV_WIDTH, CONV_CH)), const((1, CONV_CH)), const((1, CONV_CH)), const((1, CONV_CH)),
                  const((CONV_CH, d)), const((1, d))],
        out_specs=pl.BlockSpec((1, ts, d), lambda bi, si: (bi, si, 0)),
        out_shape=jax.ShapeDtypeStruct((b, s, d), BF16),
        scratch_shapes=[pltpu.VMEM((CONV_HALO + ts, CONV_CH), F32)],
        compiler_params=_params(("parallel", "parallel")),
    )(u3, u3, gc3, conv_w, conv_b, ln_g, ln_b, wpw_b, bpw)


def _attn_kernel(slopes_ref, q_ref, k_ref, vt_ref, km_ref, o_ref, sel_ref, *, nb, heads):
    blk = MOBA_BLOCK
    hg = pl.program_id(1)
    qb = pl.program_id(2)
    heads_per_vreg = LANES // HEAD_DIM
    lane = lax.broadcasted_iota(jnp.int32, (blk, LANES), 1)
    kk = lax.broadcasted_iota(jnp.int32, (blk, blk), 0)
    qq = lax.broadcasted_iota(jnp.int32, (blk, blk), 1)
    dloc = (qq - kk).astype(F32)
    biota = lax.broadcasted_iota(jnp.int32, (nb, blk), 0)
    own = pl.ds(pl.multiple_of(qb * blk, blk), blk)
    k_own = k_ref[0, own, :]
    vt_own = vt_ref[0, :, own]
    lanes_of = lambda h: slice((h // heads_per_vreg) * LANES, (h // heads_per_vreg + 1) * LANES)
    rows_of = lambda h: slice(h * HEAD_DIM, (h + 1) * HEAD_DIM)
    qhs, biases, slopes, init = [], [], [], []
    for h in range(heads):
        slope = slopes_ref[hg * heads + h]
        h2 = h % heads_per_vreg
        hmask = (lane >= HEAD_DIM * h2) & (lane < HEAD_DIM * (h2 + 1))
        q2 = q_ref[0, :, lanes_of(h)]
        qh = jnp.where(hmask, q2, jnp.zeros_like(q2))

        kmh, kml = _split_bf16(km_ref[0, :, lanes_of(h)])
        bs = _nt_dot(kmh, qh) + _nt_dot(kml, qh)
        bs = jnp.where(biota < qb, bs, NEG_INF)
        sel = jnp.zeros((nb, blk), F32)
        for _ in range(MOBA_TOPK):
            m = jnp.max(bs, axis=0, keepdims=True)
            idx = jnp.min(jnp.where(bs == m, biota, nb), axis=0, keepdims=True)
            hit = biota == idx
            sel = jnp.where(hit & (m > NEG_INF), 1.0, sel)
            bs = jnp.where(hit, NEG_INF, bs)
        sel_ref[h] = jnp.where(sel > 0.5, 0.0, NEG_INF)

        bias = -slope * dloc
        s = _nt_dot(k_own[:, lanes_of(h)], qh) + bias
        s = jnp.where(kk <= qq, s, NEG_INF)
        m0 = jnp.max(s, axis=0, keepdims=True)
        p = jnp.exp(s - m0)
        l0 = jnp.sum(p, axis=0, keepdims=True)
        acc0 = jnp.dot(vt_own[rows_of(h), :], p.astype(BF16), preferred_element_type=F32)
        qhs.append(qh)
        biases.append(bias)
        slopes.append(slope)
        init += [m0, l0, acc0]

    grp = ATTN_KV_GROUP

    def body(it, carry):
        j0 = it * grp
        off = pl.multiple_of(j0 * blk, grp * blk)
        kj = k_ref[0, pl.ds(off, grp * blk), :]
        vtj = vt_ref[0, :, pl.ds(off, grp * blk)]
        new = []
        for h in range(heads):
            m, l, acc = carry[3 * h:3 * h + 3]
            sfull = _nt_dot(kj[:, lanes_of(h)], qhs[h])
            ss = []
            for g in range(grp):
                dj = ((qb - j0 - g) * blk).astype(F32)
                mrow = sel_ref[h, pl.ds(j0 + g, 1), :] - slopes[h] * dj
                ss.append(sfull[g * blk:(g + 1) * blk, :] + biases[h] + mrow)
            m_new = functools.reduce(jnp.maximum, [m] + [jnp.max(s, axis=0, keepdims=True) for s in ss])
            alpha = jnp.exp(m - m_new)
            ps = [jnp.exp(s - m_new) for s in ss]
            l = alpha * l + functools.reduce(lambda a, b_: a + b_, [jnp.sum(p, axis=0, keepdims=True) for p in ps])
            pcat = jnp.concatenate([p.astype(BF16) for p in ps], axis=0)
            acc = alpha * acc + jnp.dot(vtj[rows_of(h), :], pcat, preferred_element_type=F32)
            new += [m_new, l, acc]
        return tuple(new)

    fin = lax.fori_loop(0, (qb + grp - 1) // grp, body, tuple(init))
    ot = jnp.concatenate([fin[3 * h + 2] / fin[3 * h + 1] for h in range(heads)], axis=0)
    o_ref[0] = ot.T.astype(BF16)


def _attn(q3, k3, vt3, km3, slopes, heads):
    b, s, w = q3.shape
    nb = s // MOBA_BLOCK
    hw = heads * HEAD_DIM
    assert nb % ATTN_KV_GROUP == 0 and hw % LANES == 0 and w % hw == 0
    return pl.pallas_call(
        functools.partial(_attn_kernel, nb=nb, heads=heads),
        grid=(b, w // hw, nb),
        in_specs=[pl.BlockSpec(memory_space=pltpu.SMEM),
                  pl.BlockSpec((1, MOBA_BLOCK, hw), lambda bi, hg, qb: (bi, qb, hg)),
                  pl.BlockSpec((1, s, hw), lambda bi, hg, qb: (bi, 0, hg)),
                  pl.BlockSpec((1, hw, s), lambda bi, hg, qb: (bi, hg, 0)),
                  pl.BlockSpec((1, nb, hw), lambda bi, hg, qb: (bi, 0, hg))],
        out_specs=pl.BlockSpec((1, MOBA_BLOCK, hw), lambda bi, hg, qb: (bi, qb, hg)),
        out_shape=jax.ShapeDtypeStruct((b, s, w), BF16),
        scratch_shapes=[pltpu.VMEM((heads, nb, MOBA_BLOCK), F32)],
        compiler_params=_params(("parallel", "parallel", "arbitrary")),
    )(slopes, q3, k3, vt3, km3)


def _merge_kernel(at_ref, cm_ref, ga_ref, x_ref, wao_ref, wo_ref, g2_ref, wq_ref, x1_ref, h2_ref, qp_ref):
    ya = jnp.dot(at_ref[...], wao_ref[...], preferred_element_type=F32)
    mix = cm_ref[...].astype(F32) + jax.nn.sigmoid(ga_ref[...].astype(F32)) * ya
    x1 = x_ref[...] + jnp.dot(mix.astype(BF16), wo_ref[...], preferred_element_type=F32)
    x1_ref[...] = x1
    h2 = x1 * lax.rsqrt(jnp.mean(x1 * x1, axis=-1, keepdims=True) + NORM_EPS) * g2_ref[...]
    h2b = h2.astype(BF16)
    h2_ref[...] = h2b
    qp_ref[...] = jnp.dot(h2b, wq_ref[...], preferred_element_type=F32)


def _merge(at2, cm2, ga2, x2, wao_b, wo_b, g2, wq_b, tm):
    t, d = x2.shape
    nq = wq_b.shape[1]
    tok = lambda w: pl.BlockSpec((tm, w), lambda i: (i, 0))
    const = lambda shape: pl.BlockSpec(shape, lambda i: (0,) * len(shape))
    return pl.pallas_call(
        _merge_kernel,
        grid=(t // tm,),
        in_specs=[tok(ATTN_W), tok(d), tok(d), tok(d),
                  const((ATTN_W, d)), const((d, d)), const((1, d)), const((d, nq))],
        out_specs=[tok(d), tok(d), tok(nq)],
        out_shape=[jax.ShapeDtypeStruct((t, d), F32),
                   jax.ShapeDtypeStruct((t, d), BF16),
                   jax.ShapeDtypeStruct((t, nq), F32)],
        compiler_params=_params(("parallel",)),
    )(at2, cm2, ga2, x2, wao_b, wo_b, g2, wq_b)


def _top_ranked(x, kiota, exact_ties):
    rank = jnp.full(x.shape, NOT_RANKED, F32)
    vals = []
    for r in range(PEER_TOPK):
        m = jnp.max(x, axis=0, keepdims=True)
        hit = x == m
        if exact_ties:
            idx = jnp.min(jnp.where(hit, kiota, float(x.shape[0])), axis=0, keepdims=True)
            hit = kiota == idx
        x = jnp.where(hit, NEG_INF, x)
        rank = jnp.where(hit, float(r), rank)
        vals.append(m)
    return rank, vals


def _peer_select(s1, s2, kiota, exact_ties):
    k = PEER_TOPK
    g = SUBLANES
    tt = s1.shape[1]
    rank1, v1 = _top_ranked(s1, kiota, exact_ties)
    rank2, v2 = _top_ranked(s2, kiota, exact_ties)

    sub = lax.broadcasted_iota(jnp.int32, (g, tt), 0).astype(F32)
    v2a = jnp.concatenate(v2[:g], axis=0)
    v2b = jnp.concatenate(v2[g:], axis=0)
    v1b = jnp.concatenate(v1[g:], axis=0)
    cands = [v1[0] + v2a, v1[0] + v2b] + [v1[r] + v2a for r in range(1, g)] + [v1b + v2[0]]
    poss = [sub, sub + float(g)] + [sub + float(r * k) for r in range(1, g)] + [(sub + float(g)) * float(k)]
    orig = list(cands)
    picked = [jnp.zeros((g, tt), F32) for _ in cands]
    for _ in range(k):
        m = functools.reduce(jnp.maximum, cands)
        m = jnp.max(m, axis=0, keepdims=True)
        hits = [c == m for c in cands]
        if exact_ties:
            pm = functools.reduce(jnp.minimum, [jnp.where(h, p, 1e9) for h, p in zip(hits, poss)])
            pm = jnp.min(pm, axis=0, keepdims=True)
            hits = [p == pm for p in poss]
        cands = [jnp.where(h, NEG_INF, c) for h, c in zip(hits, cands)]
        picked = [jnp.where(h, 1.0, s) for h, s in zip(hits, picked)]
    cmax = v1[0] + v2[0]
    z = functools.reduce(
        lambda a, b_: a + b_,
        [jnp.sum(jnp.where(s > 0.5, jnp.exp(c - cmax), 0.0), axis=0, keepdims=True) for s, c in zip(picked, orig)])
    lrow = [jnp.sum(picked[0] + picked[1], axis=0, keepdims=True)]
    lrow += [jnp.sum(picked[r + 1], axis=0, keepdims=True) for r in range(1, g)]
    lvec = jnp.concatenate(lrow + [picked[g + 1]], axis=0)

    l1 = jnp.zeros(s1.shape, F32)
    for r in range(k):
        l1 = jnp.where(rank1 == float(r), lvec[r:r + 1, :], l1)
    ranked = lambda rk: jnp.sum(jnp.where(rk < float(k), 1.0, 0.0), axis=0, keepdims=True)
    counts = jnp.concatenate([ranked(rank1), ranked(rank2), jnp.sum(lvec, axis=0, keepdims=True)], axis=0)
    e2 = jnp.exp(s2 - v2[0])
    c1 = jnp.exp(s1 - v1[0]) / z
    return rank2, e2, l1, c1, counts


def _peer_topk_kernel(qp_ref, k1_ref, k2_ref, r2_ref, e2_ref, l1_ref, c1_ref):
    half = PEER_QDIM // 2
    tt = qp_ref.shape[0]
    qp = qp_ref[...]
    kiota = lax.broadcasted_iota(jnp.int32, (PEER_NKEYS, tt), 0).astype(F32)

    def scores(keys, qside):
        kh, kl = _split_bf16(keys)
        qh, ql = _split_bf16(qside)
        return _nt_dot(kh, qh) + (_nt_dot(kh, ql) + _nt_dot(kl, qh))

    s1 = scores(k1_ref[...], qp[:, :half])
    s2 = scores(k2_ref[...], qp[:, half:])
    fast = _peer_select(s1, s2, kiota, exact_ties=False)
    tied = jnp.max(jnp.abs(fast[4] - float(PEER_TOPK))) > 0.0
    rank2, e2, l1, c1, _ = lax.cond(tied, lambda: _peer_select(s1, s2, kiota, exact_ties=True), lambda: fast)
    r2_ref[0] = rank2.astype(BF16)
    e2_ref[0] = e2.astype(BF16)
    l1_ref[0] = l1
    c1_ref[0] = c1


def _peer_topk(qp, keys1, keys2, tt):
    t = qp.shape[0]
    shape = (PEER_HEADS, PEER_NKEYS, t)
    ospec = pl.BlockSpec((1, PEER_NKEYS, tt), lambda i, h: (h, 0, i))
    kspec = pl.BlockSpec(keys1.shape, lambda i, h: (0, 0))
    return pl.pallas_call(
        _peer_topk_kernel,
        grid=(t // tt, PEER_HEADS),
        in_specs=[pl.BlockSpec((tt, PEER_QDIM), lambda i, h: (i, h)), kspec, kspec],
        out_specs=[ospec] * 4,
        out_shape=[jax.ShapeDtypeStruct(shape, BF16), jax.ShapeDtypeStruct(shape, BF16),
                   jax.ShapeDtypeStruct(shape, F32), jax.ShapeDtypeStruct(shape, F32)],
        compiler_params=_params(("parallel", "parallel")),
    )(qp, keys1, keys2)


def _peer_gate(at_ref, ga_ref, r2_ref, e2_ref, l1_ref, c1_ref, n1_range):
    tt = at_ref.shape[1]
    outs = []
    for a in n1_range:
        rows = slice(a * PEER_NKEYS, (a + 1) * PEER_NKEYS)
        gate = None
        for h in range(PEER_HEADS):
            cnt = jnp.broadcast_to(l1_ref[h, a:a + 1, :], (PEER_NKEYS, tt)).astype(BF16)
            c1 = jnp.broadcast_to(c1_ref[h, a:a + 1, :], (PEER_NKEYS, tt)).astype(BF16)
            e2 = e2_ref[h]
            term = jnp.where(r2_ref[h] < cnt, e2, jnp.zeros_like(e2)) * c1
            gate = term if gate is None else gate + term
        slab = at_ref[rows, :]
        act = 0.5 * slab * (1.0 + lax.erf(slab * math.sqrt(0.5)))
        outs.append(gate * act.astype(BF16))
    return outs


def _peer_dense_kernel(h2_ref, u_ref, vt_ref, r2_ref, e2_ref, l1_ref, c1_ref, x1_ref, gf_ref, o_ref,
                       at0_ref, at1_ref, ga0_ref, ga1_ref, acc_ref, *, n_exp_tiles, n1_per_tile):
    s = pl.program_id(0)
    j3 = jnp.maximum(s - 2, 0) % n_exp_tiles

    @pl.when(s == 0)
    def _():
        for ref in (at0_ref, at1_ref, ga0_ref, ga1_ref):
            ref[...] = jnp.zeros_like(ref)

    @pl.when(j3 == 0)
    def _():
        acc_ref[...] = jnp.zeros_like(acc_ref)

    def stages(at_w, at_r, ga_w, ga_r):
        n1_per_chunk = PEER_EXP_CHUNK // PEER_NKEYS
        gas = _peer_gate(at_r, ga_w, r2_ref, e2_ref, l1_ref, c1_ref, range(n1_per_tile))
        total = jnp.dot(vt_ref[...], ga_r[...], preferred_element_type=F32)
        at_new = _nt_dot(u_ref[...], h2_ref[...])
        acc_ref[...] += total
        ga_w[...] = jnp.concatenate(gas, axis=0)
        at_w[...] = at_new

    @pl.when(s % 2 == 0)
    def _():
        stages(at0_ref, at1_ref, ga1_ref, ga0_ref)

    @pl.when(s % 2 == 1)
    def _():
        stages(at1_ref, at0_ref, ga0_ref, ga1_ref)

    @pl.when((s >= 2) & (j3 == n_exp_tiles - 1))
    def _():
        x2 = x1_ref[...] + acc_ref[...].T
        o_ref[...] = x2 * lax.rsqrt(jnp.mean(x2 * x2, axis=-1, keepdims=True) + NORM_EPS) * gf_ref[...]


def _peer_dense(h2, u_b, vt_b, r2, e2, l1, c1, x1, gf, tt, w):
    t, d = h2.shape
    ne = u_b.shape[0]
    n1_per_tile = w // PEER_NKEYS
    nj = ne // w
    n = (t // tt) * nj

    def stage(lag):
        def split(s):
            p = jnp.clip(s - lag, 0, n - 1)
            return p // nj, p % nj
        return split

    s1, s2, s3 = stage(0), stage(1), stage(2)
    key_spec = pl.BlockSpec((PEER_HEADS, PEER_NKEYS, tt), lambda s: (0, 0, s2(s)[0]))
    n1_spec = pl.BlockSpec((PEER_HEADS, n1_per_tile, tt), lambda s: (0, s2(s)[1], s2(s)[0]))
    return pl.pallas_call(
        functools.partial(_peer_dense_kernel, n_exp_tiles=nj, n1_per_tile=n1_per_tile),
        grid=(n + 2,),
        in_specs=[pl.BlockSpec((tt, d), lambda s: (s1(s)[0], 0)),
                  pl.BlockSpec((w, d), lambda s: (s1(s)[1], 0)),
                  pl.BlockSpec((d, w), lambda s: (0, s3(s)[1])),
                  key_spec, key_spec, n1_spec, n1_spec,
                  pl.BlockSpec((tt, d), lambda s: (s3(s)[0], 0)),
                  pl.BlockSpec((1, d), lambda s: (0, 0))],
        out_specs=pl.BlockSpec((tt, d), lambda s: (s3(s)[0], 0)),
        out_shape=jax.ShapeDtypeStruct((t, d), F32),
        scratch_shapes=[pltpu.VMEM((w, tt), F32), pltpu.VMEM((w, tt), F32),
                        pltpu.VMEM((w, tt), BF16), pltpu.VMEM((w, tt), BF16),
                        pltpu.VMEM((d, tt), F32)],
        compiler_params=_params(("arbitrary",)),
    )(h2, u_b, vt_b, r2, e2, l1, c1, x1, gf)


def _alibi_slopes(n_heads):
    return jnp.asarray([2.0 ** (-8.0 * (i + 1) / n_heads) for i in range(n_heads)], dtype=F32)


def _layer(x2, b, s, g1, w_in, conv_w, conv_b, ln_g, ln_b, w_pw, b_pw, w_ao, w_o, g2, w_q, keys1, keys2, u_tab, v_tab, gf):
    t, d = x2.shape
    row = lambda a: a.reshape(1, -1).astype(F32)
    u, q, k, vt, gc, ga, km = _inproj(x2, row(g1), w_in.astype(BF16), tm=TOK_TILE, seq=s)
    cm = _conv(u.reshape(b, s, -1), gc.reshape(b, s, d), conv_w, row(conv_b), row(ln_g), row(ln_b),
               w_pw.astype(BF16), row(b_pw), ts=TOK_TILE)
    at = _attn(q.reshape(b, s, ATTN_W), k.reshape(b, s, ATTN_W), vt,
               km.reshape(b, s // MOBA_BLOCK, ATTN_W), _alibi_slopes(ATTN_HEADS), heads=ATTN_HEADS_PER_STEP)
    x1, h2, qp = _merge(at.reshape(t, ATTN_W), cm.reshape(t, d), ga, x2, w_ao.astype(BF16), w_o.astype(BF16),
                        row(g2), w_q.astype(BF16), tm=TOK_TILE)
    r2, e2, l1, c1 = _peer_topk(qp, keys1, keys2, tt=TOPK_TOK_TILE)
    return _peer_dense(h2, u_tab.astype(BF16), v_tab.astype(BF16).T, r2, e2, l1, c1, x1, gf,
                       tt=PEER_TOK_TILE, w=PEER_EXP_TILE)


def kernel(x, g_norm1, w_in, conv_w, conv_b, conv_ln_g, conv_ln_b, w_conv_pw, b_conv_pw, w_attn_out, w_out,
           g_norm2, w_peer_q, peer_keys1, peer_keys2, peer_u, peer_v, g_final):
    b, s, d = x.shape
    depth = w_in.shape[0]
    assert depth == 1, "the final RMSNorm is fused into the last layer's PEER kernel"
    assert s % TOK_TILE == 0 and d % LANES == 0
    x2 = x.reshape(b * s, d)
    l = 0
    out = _layer(x2, b, s, g_norm1[l], w_in[l], conv_w[l], conv_b[l], conv_ln_g[l], conv_ln_b[l], w_conv_pw[l],
                 b_conv_pw[l], w_attn_out[l], w_out[l], g_norm2[l], w_peer_q[l], peer_keys1[l], peer_keys2[l],
                 peer_u[l], peer_v[l], g_final.reshape(1, -1).astype(F32))
    return out.reshape(b, s, d)
```

```python
import functools
import math

import jax
import jax.numpy as jnp
from jax import lax
from jax.experimental import pallas as pl
from jax.experimental.pallas import tpu as pltpu

F32 = jnp.float32
BF16 = jnp.bfloat16

CONV_CH = 512
CONV_WIDTH = 31
ATTN_HEADS = 8
HEAD_DIM = 64
ATTN_W = ATTN_HEADS * HEAD_DIM
MOBA_BLOCK = 256
MOBA_TOPK = 3
PEER_HEADS = 8
PEER_NKEYS = 128
PEER_QDIM = 256
PEER_TOPK = 16
NORM_EPS = 1e-6

LANES = 128
SUBLANES = 8
CONV_HALO = 32
VMEM_LIMIT = 56 * 1024 * 1024

TOK_TILE = 512
TOPK_TOK_TILE = 256
PEER_TOK_TILE = 512
PEER_EXP_TILE = 2048
ATTN_KV_GROUP = 2
ATTN_HEADS_PER_STEP = 8

NEG_INF = float("-inf")
NOT_RANKED = 99.0


def _params(sem):
    return pltpu.CompilerParams(dimension_semantics=sem, vmem_limit_bytes=VMEM_LIMIT)


def _nt_dot(a, b):
    return lax.dot_general(a, b, (((1,), (1,)), ((), ())), preferred_element_type=F32)


def _split_bf16(x):
    hi = x.astype(BF16)
    lo = (x - hi.astype(F32)).astype(BF16)
    return hi, lo


def _inproj_kernel(x_ref, g_ref, w_ref, u_ref, q_ref, k_ref, vt_ref, gc_ref, ga_ref, km_ref, *, d_model):
    x = x_ref[...]
    h = x * lax.rsqrt(jnp.mean(x * x, axis=-1, keepdims=True) + NORM_EPS) * g_ref[...]
    hb = h.astype(BF16)

    def proj(a, b):
        return jnp.dot(hb, w_ref[:, a:b], preferred_element_type=F32)

    c0 = 2 * CONV_CH
    u_ref[...] = proj(0, c0).astype(BF16)
    q_ref[...] = (proj(c0, c0 + ATTN_W) * (HEAD_DIM ** -0.5)).astype(BF16)
    kf = proj(c0 + ATTN_W, c0 + 2 * ATTN_W)
    k_ref[...] = kf.astype(BF16)
    vt_ref[0] = proj(c0 + 2 * ATTN_W, c0 + 3 * ATTN_W).T.astype(BF16)
    gc_ref[...] = proj(c0 + 3 * ATTN_W, c0 + 3 * ATTN_W + d_model).astype(BF16)
    ga_ref[...] = proj(c0 + 3 * ATTN_W + d_model, c0 + 3 * ATTN_W + 2 * d_model).astype(BF16)
    for blk in range(kf.shape[0] // MOBA_BLOCK):
        km_ref[blk] = jnp.mean(kf[blk * MOBA_BLOCK:(blk + 1) * MOBA_BLOCK], axis=0, keepdims=True)


def _inproj(x2, g1, w_in_b, tm, seq):
    t, d = x2.shape
    n_in = w_in_b.shape[1]
    tiles_per_seq = seq // tm
    tok = lambda w: pl.BlockSpec((tm, w), lambda i: (i, 0))
    const = lambda shape: pl.BlockSpec(shape, lambda i: (0,) * len(shape))
    nbt = tm // MOBA_BLOCK
    return pl.pallas_call(
        functools.partial(_inproj_kernel, d_model=d),
        grid=(t // tm,),
        in_specs=[tok(d), const((1, d)), const((d, n_in))],
        out_specs=[tok(2 * CONV_CH), tok(ATTN_W), tok(ATTN_W),
                   pl.BlockSpec((1, ATTN_W, tm), lambda i: (i // tiles_per_seq, 0, i % tiles_per_seq)),
                   tok(d), tok(d),
                   pl.BlockSpec((nbt, 1, ATTN_W), lambda i: (i, 0, 0))],
        out_shape=[jax.ShapeDtypeStruct((t, 2 * CONV_CH), BF16),
                   jax.ShapeDtypeStruct((t, ATTN_W), BF16),
                   jax.ShapeDtypeStruct((t, ATTN_W), BF16),
                   jax.ShapeDtypeStruct((t // seq, ATTN_W, seq), BF16),
                   jax.ShapeDtypeStruct((t, d), BF16),
                   jax.ShapeDtypeStruct((t, d), BF16),
                   jax.ShapeDtypeStruct((t // MOBA_BLOCK, 1, ATTN_W), F32)],
        compiler_params=_params(("parallel",)),
    )(x2, g1, w_in_b)


def _conv_kernel(u_ref, halo_ref, gc_ref, cw_ref, cb_ref, lg_ref, lb_ref, wpw_ref, bpw_ref, o_ref, zext_ref, *, ts):
    def glu(u):
        u = u.astype(F32)
        return u[:, :CONV_CH] * jax.nn.sigmoid(u[:, CONV_CH:])

    zh = glu(halo_ref[0])
    zh = jnp.where(pl.program_id(1) == 0, jnp.zeros_like(zh), zh)
    zext_ref[0:CONV_HALO, :] = zh
    zext_ref[CONV_HALO:CONV_HALO + ts, :] = glu(u_ref[0])
    acc = jnp.zeros((ts, CONV_CH), F32) + cb_ref[...]
    base = CONV_HALO - (CONV_WIDTH - 1)
    for w in range(CONV_WIDTH):
        acc = acc + zext_ref[base + w:base + w + ts, :] * cw_ref[w:w + 1, :]
    mu = jnp.mean(acc, axis=-1, keepdims=True)
    xc = acc - mu
    var = jnp.mean(xc * xc, axis=-1, keepdims=True)
    y = xc * lax.rsqrt(var + NORM_EPS) * lg_ref[...] + lb_ref[...]
    y = y * jax.nn.sigmoid(y)
    yc = jnp.dot(y.astype(BF16), wpw_ref[...], preferred_element_type=F32) + bpw_ref[...]
    o_ref[0] = (jax.nn.sigmoid(gc_ref[0].astype(F32)) * yc).astype(BF16)


def _conv(u3, gc3, conv_w, conv_b, ln_g, ln_b, wpw_b, bpw, ts):
    b, s, d2 = u3.shape
    d = gc3.shape[-1]
    hpb = ts // CONV_HALO
    const = lambda shape: pl.BlockSpec(shape, lambda bi, si: (0,) * len(shape))
    return pl.pallas_call(
        functools.partial(_conv_kernel, ts=ts),
        grid=(b, s // ts),
        in_specs=[pl.BlockSpec((1, ts, d2), lambda bi, si: (bi, si, 0)),
                  pl.BlockSpec((1, CONV_HALO, d2), lambda bi, si: (bi, jnp.maximum(si * hpb - 1, 0), 0)),
                  pl.BlockSpec((1, ts, d), lambda bi, si: (bi, si, 0)),
                  const((CONV_WIDTH, CONV_CH)), const((1, CONV_CH)), const((1, CONV_CH)), const((1, CONV_CH)),
                  const((CONV_CH, d)), const((1, d))],
        out_specs=pl.BlockSpec((1, ts, d), lambda bi, si: (bi, si, 0)),
        out_shape=jax.ShapeDtypeStruct((b, s, d), BF16),
        scratch_shapes=[pltpu.VMEM((CONV_HALO + ts, CONV_CH), F32)],
        compiler_params=_params(("parallel", "parallel")),
    )(u3, u3, gc3, conv_w, conv_b, ln_g, ln_b, wpw_b, bpw)


def _attn_kernel(slopes_ref, q_ref, k_ref, vt_ref, km_ref, o_ref, sel_ref, *, nb, heads):
    blk = MOBA_BLOCK
    hg = pl.program_id(1)
    qb = pl.program_id(2)
    heads_per_vreg = LANES // HEAD_DIM
    lane = lax.broadcasted_iota(jnp.int32, (blk, LANES), 1)
    kk = lax.broadcasted_iota(jnp.int32, (blk, blk), 0)
    qq = lax.broadcasted_iota(jnp.int32, (blk, blk), 1)
    dloc = (qq - kk).astype(F32)
    biota = lax.broadcasted_iota(jnp.int32, (nb, blk), 0)
    own = pl.ds(pl.multiple_of(qb * blk, blk), blk)
    k_own = k_ref[0, own, :]
    vt_own = vt_ref[0, :, own]
    lanes_of = lambda h: slice((h // heads_per_vreg) * LANES, (h // heads_per_vreg + 1) * LANES)
    rows_of = lambda h: slice(h * HEAD_DIM, (h + 1) * HEAD_DIM)
    hs = range(heads)
    slopes = [slopes_ref[hg * heads + h] for h in hs]
    qhs = []
    for h in hs:
        h2 = h % heads_per_vreg
        hmask = (lane >= HEAD_DIM * h2) & (lane < HEAD_DIM * (h2 + 1))
        q2 = q_ref[0, :, lanes_of(h)]
        qhs.append(jnp.where(hmask, q2, jnp.zeros_like(q2)))

    kms = [_split_bf16(km_ref[0, :, lanes_of(h)]) for h in hs]
    bss = [_nt_dot(kms[h][0], qhs[h]) + _nt_dot(kms[h][1], qhs[h]) for h in hs]
    bss = [jnp.where(biota < qb, bs, NEG_INF) for bs in bss]
    sels = [jnp.zeros((nb, blk), F32) for _ in hs]
    for _ in range(MOBA_TOPK):
        ms = [jnp.max(bs, axis=0, keepdims=True) for bs in bss]
        idxs = [jnp.min(jnp.where(bs == m, biota, nb), axis=0, keepdims=True) for bs, m in zip(bss, ms)]
        hits = [biota == idx for idx in idxs]
        sels = [jnp.where(hit & (m > NEG_INF), 1.0, sel) for hit, m, sel in zip(hits, ms, sels)]
        bss = [jnp.where(hit, NEG_INF, bs) for hit, bs in zip(hits, bss)]

    biases = [-slope * dloc for slope in slopes]
    ss = [_nt_dot(k_own[:, lanes_of(h)], qhs[h]) + biases[h] for h in hs]
    ss = [jnp.where(kk <= qq, s, NEG_INF) for s in ss]
    m0s = [jnp.max(s, axis=0, keepdims=True) for s in ss]
    ps = [jnp.exp(s - m0) for s, m0 in zip(ss, m0s)]
    l0s = [jnp.sum(p, axis=0, keepdims=True) for p in ps]
    acc0s = [jnp.dot(vt_own[rows_of(h), :], ps[h].astype(BF16), preferred_element_type=F32) for h in hs]
    for h in hs:
        sel_ref[h] = jnp.where(sels[h] > 0.5, 0.0, NEG_INF)

    grp = ATTN_KV_GROUP

    def body(it, carry):
        ms, ls, accs = carry
        j0 = it * grp
        off = pl.multiple_of(j0 * blk, grp * blk)
        kj = k_ref[0, pl.ds(off, grp * blk), :]
        vtj = vt_ref[0, :, pl.ds(off, grp * blk)]
        mrows = [[sel_ref[h, pl.ds(j0 + g, 1), :] - slopes[h] * ((qb - j0 - g) * blk).astype(F32)
                  for g in range(grp)] for h in hs]
        sfull = [_nt_dot(kj[:, lanes_of(h)], qhs[h]) for h in hs]
        ss = [[sfull[h][g * blk:(g + 1) * blk, :] + biases[h] + mrows[h][g] for g in range(grp)] for h in hs]
        m_new = [functools.reduce(jnp.maximum, [ms[h]] + [jnp.max(s, axis=0, keepdims=True) for s in ss[h]])
                 for h in hs]
        alpha = [jnp.exp(ms[h] - m_new[h]) for h in hs]
        ps = [[jnp.exp(s - m_new[h]) for s in ss[h]] for h in hs]
        psum = [functools.reduce(lambda a, b_: a + b_, [jnp.sum(p, axis=0, keepdims=True) for p in ps[h]])
                for h in hs]
        l_new = [alpha[h] * ls[h] + psum[h] for h in hs]
        pcat = [jnp.concatenate([p.astype(BF16) for p in ps[h]], axis=0) for h in hs]
        pv = [jnp.dot(vtj[rows_of(h), :], pcat[h], preferred_element_type=F32) for h in hs]
        acc_new = [alpha[h] * accs[h] + pv[h] for h in hs]
        return tuple(m_new), tuple(l_new), tuple(acc_new)

    _, l_fin, acc_fin = lax.fori_loop(0, (qb + grp - 1) // grp, body, (tuple(m0s), tuple(l0s), tuple(acc0s)))
    ot = jnp.concatenate([acc_fin[h] / l_fin[h] for h in hs], axis=0)
    o_ref[0] = ot.T.astype(BF16)


def _attn(q3, k3, vt3, km3, slopes, heads):
    b, s, w = q3.shape
    nb = s // MOBA_BLOCK
    hw = heads * HEAD_DIM
    assert nb % ATTN_KV_GROUP == 0 and hw % LANES == 0 and w % hw == 0
    return pl.pallas_call(
        functools.partial(_attn_kernel, nb=nb, heads=heads),
        grid=(b, w // hw, nb),
        in_specs=[pl.BlockSpec(memory_space=pltpu.SMEM),
                  pl.BlockSpec((1, MOBA_BLOCK, hw), lambda bi, hg, qb: (bi, qb, hg)),
                  pl.BlockSpec((1, s, hw), lambda bi, hg, qb: (bi, 0, hg)),
                  pl.BlockSpec((1, hw, s), lambda bi, hg, qb: (bi, hg, 0)),
                  pl.BlockSpec((1, nb, hw), lambda bi, hg, qb: (bi, 0, hg))],
        out_specs=pl.BlockSpec((1, MOBA_BLOCK, hw), lambda bi, hg, qb: (bi, qb, hg)),
        out_shape=jax.ShapeDtypeStruct((b, s, w), BF16),
        scratch_shapes=[pltpu.VMEM((heads, nb, MOBA_BLOCK), F32)],
        compiler_params=_params(("parallel", "parallel", "arbitrary")),
    )(slopes, q3, k3, vt3, km3)


def _merge_kernel(at_ref, cm_ref, ga_ref, x_ref, wao_ref, wo_ref, g2_ref, wq_ref, x1_ref, h2_ref, qp_ref):
    ya = jnp.dot(at_ref[...], wao_ref[...], preferred_element_type=F32)
    mix = cm_ref[...].astype(F32) + jax.nn.sigmoid(ga_ref[...].astype(F32)) * ya
    x1 = x_ref[...] + jnp.dot(mix.astype(BF16), wo_ref[...], preferred_element_type=F32)
    x1_ref[...] = x1
    h2 = x1 * lax.rsqrt(jnp.mean(x1 * x1, axis=-1, keepdims=True) + NORM_EPS) * g2_ref[...]
    h2b = h2.astype(BF16)
    h2_ref[...] = h2b
    qp_ref[...] = jnp.dot(h2b, wq_ref[...], preferred_element_type=F32)


def _merge(at2, cm2, ga2, x2, wao_b, wo_b, g2, wq_b, tm):
    t, d = x2.shape
    nq = wq_b.shape[1]
    tok = lambda w: pl.BlockSpec((tm, w), lambda i: (i, 0))
    const = lambda shape: pl.BlockSpec(shape, lambda i: (0,) * len(shape))
    return pl.pallas_call(
        _merge_kernel,
        grid=(t // tm,),
        in_specs=[tok(ATTN_W), tok(d), tok(d), tok(d),
                  const((ATTN_W, d)), const((d, d)), const((1, d)), const((d, nq))],
        out_specs=[tok(d), tok(d), tok(nq)],
        out_shape=[jax.ShapeDtypeStruct((t, d), F32),
                   jax.ShapeDtypeStruct((t, d), BF16),
                   jax.ShapeDtypeStruct((t, nq), F32)],
        compiler_params=_params(("parallel",)),
    )(at2, cm2, ga2, x2, wao_b, wo_b, g2, wq_b)


def _top_ranked(x, kiota, exact_ties):
    rank = jnp.full(x.shape, NOT_RANKED, F32)
    vals = []
    for r in range(PEER_TOPK):
        m = jnp.max(x, axis=0, keepdims=True)
        hit = x == m
        if exact_ties:
            idx = jnp.min(jnp.where(hit, kiota, float(x.shape[0])), axis=0, keepdims=True)
            hit = kiota == idx
        x = jnp.where(hit, NEG_INF, x)
        rank = jnp.where(hit, float(r), rank)
        vals.append(m)
    return rank, vals


def _peer_select(s1, s2, kiota, exact_ties):
    k = PEER_TOPK
    g = SUBLANES
    tt = s1.shape[1]
    rank1, v1 = _top_ranked(s1, kiota, exact_ties)
    rank2, v2 = _top_ranked(s2, kiota, exact_ties)

    sub = lax.broadcasted_iota(jnp.int32, (g, tt), 0).astype(F32)
    v2a = jnp.concatenate(v2[:g], axis=0)
    v2b = jnp.concatenate(v2[g:], axis=0)
    v1b = jnp.concatenate(v1[g:], axis=0)
    cands = [v1[0] + v2a, v1[0] + v2b] + [v1[r] + v2a for r in range(1, g)] + [v1b + v2[0]]
    poss = [sub, sub + float(g)] + [sub + float(r * k) for r in range(1, g)] + [(sub + float(g)) * float(k)]
    orig = list(cands)
    picked = [jnp.zeros((g, tt), F32) for _ in cands]
    for _ in range(k):
        m = functools.reduce(jnp.maximum, cands)
        m = jnp.max(m, axis=0, keepdims=True)
        hits = [c == m for c in cands]
        if exact_ties:
            pm = functools.reduce(jnp.minimum, [jnp.where(h, p, 1e9) for h, p in zip(hits, poss)])
            pm = jnp.min(pm, axis=0, keepdims=True)
            hits = [p == pm for p in poss]
        cands = [jnp.where(h, NEG_INF, c) for h, c in zip(hits, cands)]
        picked = [jnp.where(h, 1.0, s) for h, s in zip(hits, picked)]
    cmax = v1[0] + v2[0]
    z = functools.reduce(
        lambda a, b_: a + b_,
        [jnp.sum(jnp.where(s > 0.5, jnp.exp(c - cmax), 0.0), axis=0, keepdims=True) for s, c in zip(picked, orig)])
    lrow = [jnp.sum(picked[0] + picked[1], axis=0, keepdims=True)]
    lrow += [jnp.sum(picked[r + 1], axis=0, keepdims=True) for r in range(1, g)]
    lvec = jnp.concatenate(lrow + [picked[g + 1]], axis=0)

    l1 = jnp.zeros(s1.shape, F32)
    for r in range(k):
        l1 = jnp.where(rank1 == float(r), lvec[r:r + 1, :], l1)
    ranked = lambda rk: jnp.sum(jnp.where(rk < float(k), 1.0, 0.0), axis=0, keepdims=True)
    counts = jnp.concatenate([ranked(rank1), ranked(rank2), jnp.sum(lvec, axis=0, keepdims=True)], axis=0)
    e2 = jnp.exp(s2 - v2[0])
    c1 = jnp.exp(s1 - v1[0]) * (0.5 / z)
    return rank2, e2, l1, c1, counts


def _peer_topk_kernel(qp_ref, k1_ref, k2_ref, r2_ref, e2_ref, l1_ref, c1_ref):
    half = PEER_QDIM // 2
    tt = qp_ref.shape[0]
    qp = qp_ref[...]
    kiota = lax.broadcasted_iota(jnp.int32, (PEER_NKEYS, tt), 0).astype(F32)

    def scores(keys, qside):
        kh, kl = _split_bf16(keys)
        qh, ql = _split_bf16(qside)
        return _nt_dot(kh, qh) + (_nt_dot(kh, ql) + _nt_dot(kl, qh))

    s1 = scores(k1_ref[...], qp[:, :half])
    s2 = scores(k2_ref[...], qp[:, half:])
    fast = _peer_select(s1, s2, kiota, exact_ties=False)
    tied = jnp.max(jnp.abs(fast[4] - float(PEER_TOPK))) > 0.0
    rank2, e2, l1, c1, _ = lax.cond(tied, lambda: _peer_select(s1, s2, kiota, exact_ties=True), lambda: fast)
    r2_ref[0] = rank2.astype(BF16)
    e2_ref[0] = e2.astype(BF16)
    l1_ref[0] = l1
    c1_ref[0] = c1


def _peer_topk(qp, keys1, keys2, tt):
    t = qp.shape[0]
    shape = (PEER_HEADS, PEER_NKEYS, t)
    ospec = pl.BlockSpec((1, PEER_NKEYS, tt), lambda i, h: (h, 0, i))
    kspec = pl.BlockSpec(keys1.shape, lambda i, h: (0, 0))
    return pl.pallas_call(
        _peer_topk_kernel,
        grid=(t // tt, PEER_HEADS),
        in_specs=[pl.BlockSpec((tt, PEER_QDIM), lambda i, h: (i, h)), kspec, kspec],
        out_specs=[ospec] * 4,
        out_shape=[jax.ShapeDtypeStruct(shape, BF16), jax.ShapeDtypeStruct(shape, BF16),
                   jax.ShapeDtypeStruct(shape, F32), jax.ShapeDtypeStruct(shape, F32)],
        compiler_params=_params(("parallel", "parallel")),
    )(qp, keys1, keys2)


def _peer_gate(r2_ref, e2_ref, l1_ref, c1_ref, n1_per_tile):
    tt = r2_ref.shape[2]
    gates = []
    for a in range(n1_per_tile):
        gate = None
        for h in range(PEER_HEADS):
            cnt = jnp.broadcast_to(l1_ref[h, a:a + 1, :], (PEER_NKEYS, tt)).astype(BF16)
            c1 = jnp.broadcast_to(c1_ref[h, a:a + 1, :], (PEER_NKEYS, tt)).astype(BF16)
            e2 = e2_ref[h]
            term = jnp.where(r2_ref[h] < cnt, e2, jnp.zeros_like(e2)) * c1
            gate = term if gate is None else gate + term
        gates.append(gate)
    return gates


def _peer_dense_kernel(h2_ref, u_ref, vt_ref, r2_ref, e2_ref, l1_ref, c1_ref, x1_ref, gf_ref, o_ref,
                       acc_ref, *, n1_per_tile):
    j = pl.program_id(1)

    @pl.when(j == 0)
    def _():
        acc_ref[...] = jnp.zeros_like(acc_ref)

    gates = _peer_gate(r2_ref, e2_ref, l1_ref, c1_ref, n1_per_tile)
    at = _nt_dot(u_ref[...], h2_ref[...])
    parts = []
    for a in range(n1_per_tile):
        x = at[a * PEER_NKEYS:(a + 1) * PEER_NKEYS, :]
        act = x * (1.0 + lax.erf(x * math.sqrt(0.5)))
        parts.append(gates[a] * act.astype(BF16))
    ga = jnp.concatenate(parts, axis=0)
    acc_ref[...] += jnp.dot(vt_ref[...], ga, preferred_element_type=F32)

    @pl.when(j == pl.num_programs(1) - 1)
    def _():
        x2 = x1_ref[...] + acc_ref[...].T
        o_ref[...] = x2 * lax.rsqrt(jnp.mean(x2 * x2, axis=-1, keepdims=True) + NORM_EPS) * gf_ref[...]


def _peer_dense(h2, u_b, vt_b, r2, e2, l1, c1, x1, gf, tt, w):
    t, d = h2.shape
    ne = u_b.shape[0]
    n1_per_tile = w // PEER_NKEYS
    key_spec = pl.BlockSpec((PEER_HEADS, PEER_NKEYS, tt), lambda i, j: (0, 0, i))
    n1_spec = pl.BlockSpec((PEER_HEADS, n1_per_tile, tt), lambda i, j: (0, j, i))
    return pl.pallas_call(
        functools.partial(_peer_dense_kernel, n1_per_tile=n1_per_tile),
        grid=(t // tt, ne // w),
        in_specs=[pl.BlockSpec((tt, d), lambda i, j: (i, 0)),
                  pl.BlockSpec((w, d), lambda i, j: (j, 0)),
                  pl.BlockSpec((d, w), lambda i, j: (0, j)),
                  key_spec, key_spec, n1_spec, n1_spec,
                  pl.BlockSpec((tt, d), lambda i, j: (i, 0)),
                  pl.BlockSpec((1, d), lambda i, j: (0, 0))],
        out_specs=pl.BlockSpec((tt, d), lambda i, j: (i, 0)),
        out_shape=jax.ShapeDtypeStruct((t, d), F32),
        scratch_shapes=[pltpu.VMEM((d, tt), F32)],
        compiler_params=_params(("parallel", "arbitrary")),
    )(h2, u_b, vt_b, r2, e2, l1, c1, x1, gf)


def _alibi_slopes(n_heads):
    return jnp.asarray([2.0 ** (-8.0 * (i + 1) / n_heads) for i in range(n_heads)], dtype=F32)


def _layer(x2, b, s, g1, w_in, conv_w, conv_b, ln_g, ln_b, w_pw, b_pw, w_ao, w_o, g2, w_q, keys1, keys2, u_tab, v_tab, gf):
    t, d = x2.shape
    row = lambda a: a.reshape(1, -1).astype(F32)
    u, q, k, vt, gc, ga, km = _inproj(x2, row(g1), w_in.astype(BF16), tm=TOK_TILE, seq=s)
    cm = _conv(u.reshape(b, s, -1), gc.reshape(b, s, d), conv_w, row(conv_b), row(ln_g), row(ln_b),
               w_pw.astype(BF16), row(b_pw), ts=TOK_TILE)
    at = _attn(q.reshape(b, s, ATTN_W), k.reshape(b, s, ATTN_W), vt,
               km.reshape(b, s // MOBA_BLOCK, ATTN_W), _alibi_slopes(ATTN_HEADS), heads=ATTN_HEADS_PER_STEP)
    x1, h2, qp = _merge(at.reshape(t, ATTN_W), cm.reshape(t, d), ga, x2, w_ao.astype(BF16), w_o.astype(BF16),
                        row(g2), w_q.astype(BF16), tm=TOK_TILE)
    r2, e2, l1, c1 = _peer_topk(qp, keys1, keys2, tt=TOPK_TOK_TILE)
    return _peer_dense(h2, u_tab.astype(BF16), v_tab.astype(BF16).T, r2, e2, l1, c1, x1, gf,
                       tt=PEER_TOK_TILE, w=PEER_EXP_TILE)


def kernel(x, g_norm1, w_in, conv_w, conv_b, conv_ln_g, conv_ln_b, w_conv_pw, b_conv_pw, w_attn_out, w_out,
           g_norm2, w_peer_q, peer_keys1, peer_keys2, peer_u, peer_v, g_final):
    b, s, d = x.shape
    depth = w_in.shape[0]
    assert depth == 1, "the final RMSNorm is fused into the last layer's PEER kernel"
    assert s % TOK_TILE == 0 and d % LANES == 0
    x2 = x.reshape(b * s, d)
    l = 0
    out = _layer(x2, b, s, g_norm1[l], w_in[l], conv_w[l], conv_b[l], conv_ln_g[l], conv_ln_b[l], w_conv_pw[l],
                 b_conv_pw[l], w_attn_out[l], w_out[l], g_norm2[l], w_peer_q[l], peer_keys1[l], peer_keys2[l],
                 peer_u[l], peer_v[l], g_final.reshape(1, -1).astype(F32))
    return out.reshape(b, s, d)
```

```python
import functools
import math

import jax
import jax.numpy as jnp
from jax import lax
from jax.experimental import pallas as pl
from jax.experimental.pallas import tpu as pltpu

F32 = jnp.float32
BF16 = jnp.bfloat16

CONV_CH = 512
CONV_WIDTH = 31
ATTN_HEADS = 8
HEAD_DIM = 64
ATTN_W = ATTN_HEADS * HEAD_DIM
MOBA_BLOCK = 256
MOBA_TOPK = 3
PEER_HEADS = 8
PEER_NKEYS = 128
PEER_QDIM = 256
PEER_TOPK = 16
NORM_EPS = 1e-6

LANES = 128
SUBLANES = 8
CONV_HALO = 32
VMEM_LIMIT = 56 * 1024 * 1024

TOK_TILE = 512
TOPK_TOK_TILE = 256
PEER_TOK_TILE = 512
PEER_EXP_TILE = 2048
ATTN_KV_GROUP = 2
ATTN_HEADS_PER_STEP = 8
ATTN_STAGE_HEADS = 8

NEG_INF = float("-inf")
NOT_RANKED = 99.0


def _params(sem):
    return pltpu.CompilerParams(dimension_semantics=sem, vmem_limit_bytes=VMEM_LIMIT)


def _nt_dot(a, b):
    return lax.dot_general(a, b, (((1,), (1,)), ((), ())), preferred_element_type=F32)


def _split_bf16(x):
    hi = x.astype(BF16)
    lo = (x - hi.astype(F32)).astype(BF16)
    return hi, lo


def _inproj_kernel(x_ref, g_ref, w_ref, u_ref, q_ref, k_ref, vt_ref, gc_ref, ga_ref, km_ref, *, d_model):
    x = x_ref[...]
    h = x * lax.rsqrt(jnp.mean(x * x, axis=-1, keepdims=True) + NORM_EPS) * g_ref[...]
    hb = h.astype(BF16)

    def proj(a, b):
        return jnp.dot(hb, w_ref[:, a:b], preferred_element_type=F32)

    c0 = 2 * CONV_CH
    u_ref[...] = proj(0, c0).astype(BF16)
    q_ref[...] = (proj(c0, c0 + ATTN_W) * (HEAD_DIM ** -0.5)).astype(BF16)
    kf = proj(c0 + ATTN_W, c0 + 2 * ATTN_W)
    k_ref[...] = kf.astype(BF16)
    vt_ref[0] = proj(c0 + 2 * ATTN_W, c0 + 3 * ATTN_W).T.astype(BF16)
    gc_ref[...] = proj(c0 + 3 * ATTN_W, c0 + 3 * ATTN_W + d_model).astype(BF16)
    ga_ref[...] = proj(c0 + 3 * ATTN_W + d_model, c0 + 3 * ATTN_W + 2 * d_model).astype(BF16)
    for blk in range(kf.shape[0] // MOBA_BLOCK):
        km_ref[blk] = jnp.mean(kf[blk * MOBA_BLOCK:(blk + 1) * MOBA_BLOCK], axis=0, keepdims=True)


def _inproj(x2, g1, w_in_b, tm, seq):
    t, d = x2.shape
    n_in = w_in_b.shape[1]
    tiles_per_seq = seq // tm
    tok = lambda w: pl.BlockSpec((tm, w), lambda i: (i, 0))
    const = lambda shape: pl.BlockSpec(shape, lambda i: (0,) * len(shape))
    nbt = tm // MOBA_BLOCK
    return pl.pallas_call(
        functools.partial(_inproj_kernel, d_model=d),
        grid=(t // tm,),
        in_specs=[tok(d), const((1, d)), const((d, n_in))],
        out_specs=[tok(2 * CONV_CH), tok(ATTN_W), tok(ATTN_W),
                   pl.BlockSpec((1, ATTN_W, tm), lambda i: (i // tiles_per_seq, 0, i % tiles_per_seq)),
                   tok(d), tok(d),
                   pl.BlockSpec((nbt, 1, ATTN_W), lambda i: (i, 0, 0))],
        out_shape=[jax.ShapeDtypeStruct((t, 2 * CONV_CH), BF16),
                   jax.ShapeDtypeStruct((t, ATTN_W), BF16),
                   jax.ShapeDtypeStruct((t, ATTN_W), BF16),
                   jax.ShapeDtypeStruct((t // seq, ATTN_W, seq), BF16),
                   jax.ShapeDtypeStruct((t, d), BF16),
                   jax.ShapeDtypeStruct((t, d), BF16),
                   jax.ShapeDtypeStruct((t // MOBA_BLOCK, 1, ATTN_W), F32)],
        compiler_params=_params(("parallel",)),
    )(x2, g1, w_in_b)


def _conv_kernel(u_ref, halo_ref, gc_ref, cw_ref, cb_ref, lg_ref, lb_ref, wpw_ref, bpw_ref, o_ref, zext_ref, zs_ref,
                 *, ts):
    def glu(u):
        u = u.astype(F32)
        return u[:, :CONV_CH] * jax.nn.sigmoid(u[:, CONV_CH:])

    zh = glu(halo_ref[0])
    zh = jnp.where(pl.program_id(1) == 0, jnp.zeros_like(zh), zh)
    zext_ref[0:CONV_HALO, :] = zh
    zext_ref[CONV_HALO:CONV_HALO + ts, :] = glu(u_ref[0])
    span = zs_ref.shape[1]
    for r in range(1, SUBLANES):
        zs_ref[r - 1] = zext_ref[r:r + span, :]
    acc = jnp.zeros((ts, CONV_CH), F32) + cb_ref[...]
    base = CONV_HALO - (CONV_WIDTH - 1)
    for w in range(CONV_WIDTH):
        r = (base + w) % SUBLANES
        q = base + w - r
        tap = zext_ref[q:q + ts, :] if r == 0 else zs_ref[r - 1, q:q + ts, :]
        acc = acc + tap * cw_ref[w:w + 1, :]
    mu = jnp.mean(acc, axis=-1, keepdims=True)
    xc = acc - mu
    var = jnp.mean(xc * xc, axis=-1, keepdims=True)
    y = xc * lax.rsqrt(var + NORM_EPS) * lg_ref[...] + lb_ref[...]
    y = y * jax.nn.sigmoid(y)
    yc = jnp.dot(y.astype(BF16), wpw_ref[...], preferred_element_type=F32) + bpw_ref[...]
    o_ref[0] = (jax.nn.sigmoid(gc_ref[0].astype(F32)) * yc).astype(BF16)


def _conv(u3, gc3, conv_w, conv_b, ln_g, ln_b, wpw_b, bpw, ts):
    b, s, d2 = u3.shape
    d = gc3.shape[-1]
    hpb = ts // CONV_HALO
    const = lambda shape: pl.BlockSpec(shape, lambda bi, si: (0,) * len(shape))
    return pl.pallas_call(
        functools.partial(_conv_kernel, ts=ts),
        grid=(b, s // ts),
        in_specs=[pl.BlockSpec((1, ts, d2), lambda bi, si: (bi, si, 0)),
                  pl.BlockSpec((1, CONV_HALO, d2), lambda bi, si: (bi, jnp.maximum(si * hpb - 1, 0), 0)),
                  pl.BlockSpec((1, ts, d), lambda bi, si: (bi, si, 0)),
                  const((CONV_WIDTH, CONV_CH)), const((1, CONV_CH)), const((1, CONV_CH)), const((1, CONV_CH)),
                  const((CONV_CH, d)), const((1, d))],
        out_specs=pl.BlockSpec((1, ts, d), lambda bi, si: (bi, si, 0)),
        out_shape=jax.ShapeDtypeStruct((b, s, d), BF16),
        scratch_shapes=[pltpu.VMEM((CONV_HALO + ts, CONV_CH), F32),
                        pltpu.VMEM((SUBLANES - 1, CONV_HALO + ts - SUBLANES, CONV_CH), F32)],
        compiler_params=_params(("parallel", "parallel")),
    )(u3, u3, gc3, conv_w, conv_b, ln_g, ln_b, wpw_b, bpw)


def _attn_kernel(slopes_ref, q_ref, k_ref, vt_ref, km_ref, o_ref, sel_ref, *, nb, heads):
    blk = MOBA_BLOCK
    hg = pl.program_id(1)
    qb = pl.program_id(2)
    heads_per_vreg = LANES // HEAD_DIM
    lane = lax.broadcasted_iota(jnp.int32, (blk, LANES), 1)
    kk = lax.broadcasted_iota(jnp.int32, (blk, blk), 0)
    qq = lax.broadcasted_iota(jnp.int32, (blk, blk), 1)
    dloc = (qq - kk).astype(F32)
    biota = lax.broadcasted_iota(jnp.int32, (nb, blk), 0)
    own = pl.ds(pl.multiple_of(qb * blk, blk), blk)
    k_own = k_ref[0, own, :]
    vt_own = vt_ref[0, :, own]
    lanes_of = lambda h: slice((h // heads_per_vreg) * LANES, (h // heads_per_vreg + 1) * LANES)
    rows_of = lambda h: slice(h * HEAD_DIM, (h + 1) * HEAD_DIM)
    hs = range(heads)
    slopes = [slopes_ref[hg * heads + h] for h in hs]
    qhs = []
    for h in hs:
        h2 = h % heads_per_vreg
        hmask = (lane >= HEAD_DIM * h2) & (lane < HEAD_DIM * (h2 + 1))
        q2 = q_ref[0, :, lanes_of(h)]
        qhs.append(jnp.where(hmask, q2, jnp.zeros_like(q2)))

    kms = [_split_bf16(km_ref[0, :, lanes_of(h)]) for h in hs]
    bss = [_nt_dot(kms[h][0], qhs[h]) + _nt_dot(kms[h][1], qhs[h]) for h in hs]
    bss = [jnp.where(biota < qb, bs, NEG_INF) for bs in bss]
    sels = [jnp.zeros((nb, blk), F32) for _ in hs]
    for _ in range(MOBA_TOPK):
        ms = [jnp.max(bs, axis=0, keepdims=True) for bs in bss]
        idxs = [jnp.min(jnp.where(bs == m, biota, nb), axis=0, keepdims=True) for bs, m in zip(bss, ms)]
        hits = [biota == idx for idx in idxs]
        sels = [jnp.where(hit & (m > NEG_INF), 1.0, sel) for hit, m, sel in zip(hits, ms, sels)]
        bss = [jnp.where(hit, NEG_INF, bs) for hit, bs in zip(hits, bss)]

    biases = [-slope * dloc for slope in slopes]
    ss = [_nt_dot(k_own[:, lanes_of(h)], qhs[h]) + biases[h] for h in hs]
    ss = [jnp.where(kk <= qq, s, NEG_INF) for s in ss]
    m0s = [jnp.max(s, axis=0, keepdims=True) for s in ss]
    ps = [jnp.exp(s - m0) for s, m0 in zip(ss, m0s)]
    l0s = [jnp.sum(p, axis=0, keepdims=True) for p in ps]
    acc0s = [jnp.dot(vt_own[rows_of(h), :], ps[h].astype(BF16), preferred_element_type=F32) for h in hs]
    for h in hs:
        sel_ref[h] = jnp.where(sels[h] > 0.5, 0.0, NEG_INF)

    grp = ATTN_KV_GROUP

    def body(it, carry):
        ms, ls, accs = carry
        j0 = it * grp
        off = pl.multiple_of(j0 * blk, grp * blk)
        kj = k_ref[0, pl.ds(off, grp * blk), :]
        vtj = vt_ref[0, :, pl.ds(off, grp * blk)]
        mrows = [[sel_ref[h, pl.ds(j0 + g, 1), :] - slopes[h] * ((qb - j0 - g) * blk).astype(F32)
                  for g in range(grp)] for h in hs]
        m_new, l_new, acc_new = [None] * heads, [None] * heads, [None] * heads
        for h0 in range(0, heads, ATTN_STAGE_HEADS):
            hb = range(h0, h0 + ATTN_STAGE_HEADS)
            sfull = {h: _nt_dot(kj[:, lanes_of(h)], qhs[h]) for h in hb}
            ss = {h: [sfull[h][g * blk:(g + 1) * blk, :] + biases[h] for g in range(grp)] for h in hb}
            for h in hb:
                m_new[h] = functools.reduce(
                    jnp.maximum,
                    [ms[h]] + [jnp.max(ss[h][g], axis=0, keepdims=True) + mrows[h][g] for g in range(grp)])
            alpha = {h: jnp.exp(ms[h] - m_new[h]) for h in hb}
            ps = {h: [jnp.exp(ss[h][g] - (m_new[h] - mrows[h][g])) for g in range(grp)] for h in hb}
            psum = {h: functools.reduce(lambda a, b_: a + b_, [jnp.sum(p, axis=0, keepdims=True) for p in ps[h]])
                    for h in hb}
            for h in hb:
                l_new[h] = alpha[h] * ls[h] + psum[h]
            pcat = {h: jnp.concatenate([p.astype(BF16) for p in ps[h]], axis=0) for h in hb}
            pv = {h: jnp.dot(vtj[rows_of(h), :], pcat[h], preferred_element_type=F32) for h in hb}
            for h in hb:
                acc_new[h] = alpha[h] * accs[h] + pv[h]
        return tuple(m_new), tuple(l_new), tuple(acc_new)

    _, l_fin, acc_fin = lax.fori_loop(0, (qb + grp - 1) // grp, body, (tuple(m0s), tuple(l0s), tuple(acc0s)))
    ot = jnp.concatenate([acc_fin[h] / l_fin[h] for h in hs], axis=0)
    o_ref[0] = ot.T.astype(BF16)


def _attn(q3, k3, vt3, km3, slopes, heads):
    b, s, w = q3.shape
    nb = s // MOBA_BLOCK
    hw = heads * HEAD_DIM
    assert nb % ATTN_KV_GROUP == 0 and hw % LANES == 0 and w % hw == 0
    return pl.pallas_call(
        functools.partial(_attn_kernel, nb=nb, heads=heads),
        grid=(b, w // hw, nb),
        in_specs=[pl.BlockSpec(memory_space=pltpu.SMEM),
                  pl.BlockSpec((1, MOBA_BLOCK, hw), lambda bi, hg, qb: (bi, qb, hg)),
                  pl.BlockSpec((1, s, hw), lambda bi, hg, qb: (bi, 0, hg)),
                  pl.BlockSpec((1, hw, s), lambda bi, hg, qb: (bi, hg, 0)),
                  pl.BlockSpec((1, nb, hw), lambda bi, hg, qb: (bi, 0, hg))],
        out_specs=pl.BlockSpec((1, MOBA_BLOCK, hw), lambda bi, hg, qb: (bi, qb, hg)),
        out_shape=jax.ShapeDtypeStruct((b, s, w), BF16),
        scratch_shapes=[pltpu.VMEM((heads, nb, MOBA_BLOCK), F32)],
        compiler_params=_params(("parallel", "parallel", "arbitrary")),
    )(slopes, q3, k3, vt3, km3)


def _merge_kernel(at_ref, cm_ref, ga_ref, x_ref, wao_ref, wo_ref, g2_ref, wq_ref, x1_ref, h2_ref, qp_ref):
    ya = jnp.dot(at_ref[...], wao_ref[...], preferred_element_type=F32)
    mix = cm_ref[...].astype(F32) + jax.nn.sigmoid(ga_ref[...].astype(F32)) * ya
    x1 = x_ref[...] + jnp.dot(mix.astype(BF16), wo_ref[...], preferred_element_type=F32)
    x1_ref[...] = x1
    h2 = x1 * lax.rsqrt(jnp.mean(x1 * x1, axis=-1, keepdims=True) + NORM_EPS) * g2_ref[...]
    h2b = h2.astype(BF16)
    h2_ref[...] = h2b
    qp_ref[...] = jnp.dot(h2b, wq_ref[...], preferred_element_type=F32)


def _merge(at2, cm2, ga2, x2, wao_b, wo_b, g2, wq_b, tm):
    t, d = x2.shape
    nq = wq_b.shape[1]
    tok = lambda w: pl.BlockSpec((tm, w), lambda i: (i, 0))
    const = lambda shape: pl.BlockSpec(shape, lambda i: (0,) * len(shape))
    return pl.pallas_call(
        _merge_kernel,
        grid=(t // tm,),
        in_specs=[tok(ATTN_W), tok(d), tok(d), tok(d),
                  const((ATTN_W, d)), const((d, d)), const((1, d)), const((d, nq))],
        out_specs=[tok(d), tok(d), tok(nq)],
        out_shape=[jax.ShapeDtypeStruct((t, d), F32),
                   jax.ShapeDtypeStruct((t, d), BF16),
                   jax.ShapeDtypeStruct((t, nq), F32)],
        compiler_params=_params(("parallel",)),
    )(at2, cm2, ga2, x2, wao_b, wo_b, g2, wq_b)


def _top_ranked(x, kiota, exact_ties):
    rank = jnp.full(x.shape, NOT_RANKED, F32)
    vals = []
    for r in range(PEER_TOPK):
        m = jnp.max(x, axis=0, keepdims=True)
        hit = x == m
        if exact_ties:
            idx = jnp.min(jnp.where(hit, kiota, float(x.shape[0])), axis=0, keepdims=True)
            hit = kiota == idx
        x = jnp.where(hit, NEG_INF, x)
        rank = jnp.where(hit, float(r), rank)
        vals.append(m)
    return rank, vals


def _peer_select(s1, s2, kiota, exact_ties):
    k = PEER_TOPK
    g = SUBLANES
    tt = s1.shape[1]
    rank1, v1 = _top_ranked(s1, kiota, exact_ties)
    rank2, v2 = _top_ranked(s2, kiota, exact_ties)

    sub = lax.broadcasted_iota(jnp.int32, (g, tt), 0).astype(F32)
    v2a = jnp.concatenate(v2[:g], axis=0)
    v2b = jnp.concatenate(v2[g:], axis=0)
    v1b = jnp.concatenate(v1[g:], axis=0)
    cands = [v1[0] + v2a, v1[0] + v2b] + [v1[r] + v2a for r in range(1, g)] + [v1b + v2[0]]
    poss = [sub, sub + float(g)] + [sub + float(r * k) for r in range(1, g)] + [(sub + float(g)) * float(k)]
    orig = list(cands)
    picked = [jnp.zeros((g, tt), F32) for _ in cands]
    for _ in range(k):
        m = functools.reduce(jnp.maximum, cands)
        m = jnp.max(m, axis=0, keepdims=True)
        hits = [c == m for c in cands]
        if exact_ties:
            pm = functools.reduce(jnp.minimum, [jnp.where(h, p, 1e9) for h, p in zip(hits, poss)])
            pm = jnp.min(pm, axis=0, keepdims=True)
            hits = [p == pm for p in poss]
        cands = [jnp.where(h, NEG_INF, c) for h, c in zip(hits, cands)]
        picked = [jnp.where(h, 1.0, s) for h, s in zip(hits, picked)]
    cmax = v1[0] + v2[0]
    z = functools.reduce(
        lambda a, b_: a + b_,
        [jnp.sum(jnp.where(s > 0.5, jnp.exp(c - cmax), 0.0), axis=0, keepdims=True) for s, c in zip(picked, orig)])
    lrow = [jnp.sum(picked[0] + picked[1], axis=0, keepdims=True)]
    lrow += [jnp.sum(picked[r + 1], axis=0, keepdims=True) for r in range(1, g)]
    lvec = jnp.concatenate(lrow + [picked[g + 1]], axis=0)

    l1 = jnp.zeros(s1.shape, F32)
    for r in range(k):
        l1 = jnp.where(rank1 == float(r), lvec[r:r + 1, :], l1)
    ranked = lambda rk: jnp.sum(jnp.where(rk < float(k), 1.0, 0.0), axis=0, keepdims=True)
    counts = jnp.concatenate([ranked(rank1), ranked(rank2), jnp.sum(lvec, axis=0, keepdims=True)], axis=0)
    e2 = jnp.exp(s2 - v2[0])
    c1 = jnp.exp(s1 - v1[0]) * (0.5 / z)
    return rank2, e2, l1, c1, counts


def _peer_topk_kernel(qp_ref, k1_ref, k2_ref, r2_ref, e2_ref, l1_ref, c1_ref):
    half = PEER_QDIM // 2
    tt = qp_ref.shape[0]
    qp = qp_ref[...]
    kiota = lax.broadcasted_iota(jnp.int32, (PEER_NKEYS, tt), 0).astype(F32)

    def scores(keys, qside):
        kh, kl = _split_bf16(keys)
        qh, ql = _split_bf16(qside)
        return _nt_dot(kh, qh) + (_nt_dot(kh, ql) + _nt_dot(kl, qh))

    s1 = scores(k1_ref[...], qp[:, :half])
    s2 = scores(k2_ref[...], qp[:, half:])
    fast = _peer_select(s1, s2, kiota, exact_ties=False)
    tied = jnp.max(jnp.abs(fast[4] - float(PEER_TOPK))) > 0.0
    rank2, e2, l1, c1, _ = lax.cond(tied, lambda: _peer_select(s1, s2, kiota, exact_ties=True), lambda: fast)
    r2_ref[0] = rank2.astype(BF16)
    e2_ref[0] = e2.astype(BF16)
    l1_ref[0] = l1
    c1_ref[0] = c1


def _peer_topk(qp, keys1, keys2, tt):
    t = qp.shape[0]
    shape = (PEER_HEADS, PEER_NKEYS, t)
    ospec = pl.BlockSpec((1, PEER_NKEYS, tt), lambda i, h: (h, 0, i))
    kspec = pl.BlockSpec(keys1.shape, lambda i, h: (0, 0))
    return pl.pallas_call(
        _peer_topk_kernel,
        grid=(t // tt, PEER_HEADS),
        in_specs=[pl.BlockSpec((tt, PEER_QDIM), lambda i, h: (i, h)), kspec, kspec],
        out_specs=[ospec] * 4,
        out_shape=[jax.ShapeDtypeStruct(shape, BF16), jax.ShapeDtypeStruct(shape, BF16),
                   jax.ShapeDtypeStruct(shape, F32), jax.ShapeDtypeStruct(shape, F32)],
        compiler_params=_params(("parallel", "parallel")),
    )(qp, keys1, keys2)


def _peer_gate(r2_ref, e2_ref, l1_ref, c1_ref, n1_per_tile):
    tt = r2_ref.shape[2]
    gates = []
    for a in range(n1_per_tile):
        gate = None
        for h in range(PEER_HEADS):
            cnt = jnp.broadcast_to(l1_ref[h, a:a + 1, :], (PEER_NKEYS, tt)).astype(BF16)
            c1 = jnp.broadcast_to(c1_ref[h, a:a + 1, :], (PEER_NKEYS, tt)).astype(BF16)
            e2 = e2_ref[h]
            term = jnp.where(r2_ref[h] < cnt, e2, jnp.zeros_like(e2)) * c1
            gate = term if gate is None else gate + term
        gates.append(gate)
    return gates


def _peer_dense_kernel(h2_ref, u_ref, vt_ref, r2_ref, e2_ref, l1_ref, c1_ref, x1_ref, gf_ref, o_ref,
                       acc_ref, *, n1_per_tile):
    j = pl.program_id(1)

    @pl.when(j == 0)
    def _():
        acc_ref[...] = jnp.zeros_like(acc_ref)

    gates = _peer_gate(r2_ref, e2_ref, l1_ref, c1_ref, n1_per_tile)
    at = _nt_dot(u_ref[...], h2_ref[...])
    parts = []
    for a in range(n1_per_tile):
        x = at[a * PEER_NKEYS:(a + 1) * PEER_NKEYS, :]
        act = x * (1.0 + lax.erf(x * math.sqrt(0.5)))
        parts.append(gates[a] * act.astype(BF16))
    ga = jnp.concatenate(parts, axis=0)
    acc_ref[...] += jnp.dot(vt_ref[...], ga, preferred_element_type=F32)

    @pl.when(j == pl.num_programs(1) - 1)
    def _():
        x2 = x1_ref[...] + acc_ref[...].T
        o_ref[...] = x2 * lax.rsqrt(jnp.mean(x2 * x2, axis=-1, keepdims=True) + NORM_EPS) * gf_ref[...]


def _peer_dense(h2, u_b, vt_b, r2, e2, l1, c1, x1, gf, tt, w):
    t, d = h2.shape
    ne = u_b.shape[0]
    n1_per_tile = w // PEER_NKEYS
    key_spec = pl.BlockSpec((PEER_HEADS, PEER_NKEYS, tt), lambda i, j: (0, 0, i))
    n1_spec = pl.BlockSpec((PEER_HEADS, n1_per_tile, tt), lambda i, j: (0, j, i))
    return pl.pallas_call(
        functools.partial(_peer_dense_kernel, n1_per_tile=n1_per_tile),
        grid=(t // tt, ne // w),
        in_specs=[pl.BlockSpec((tt, d), lambda i, j: (i, 0)),
                  pl.BlockSpec((w, d), lambda i, j: (j, 0)),
                  pl.BlockSpec((d, w), lambda i, j: (0, j)),
                  key_spec, key_spec, n1_spec, n1_spec,
                  pl.BlockSpec((tt, d), lambda i, j: (i, 0)),
                  pl.BlockSpec((1, d), lambda i, j: (0, 0))],
        out_specs=pl.BlockSpec((tt, d), lambda i, j: (i, 0)),
        out_shape=jax.ShapeDtypeStruct((t, d), F32),
        scratch_shapes=[pltpu.VMEM((d, tt), F32)],
        compiler_params=_params(("parallel", "arbitrary")),
    )(h2, u_b, vt_b, r2, e2, l1, c1, x1, gf)


def _alibi_slopes(n_heads):
    return jnp.asarray([2.0 ** (-8.0 * (i + 1) / n_heads) for i in range(n_heads)], dtype=F32)


def _layer(x2, b, s, g1, w_in, conv_w, conv_b, ln_g, ln_b, w_pw, b_pw, w_ao, w_o, g2, w_q, keys1, keys2, u_tab, v_tab, gf):
    t, d = x2.shape
    row = lambda a: a.reshape(1, -1).astype(F32)
    u, q, k, vt, gc, ga, km = _inproj(x2, row(g1), w_in.astype(BF16), tm=TOK_TILE, seq=s)
    cm = _conv(u.reshape(b, s, -1), gc.reshape(b, s, d), conv_w, row(conv_b), row(ln_g), row(ln_b),
               w_pw.astype(BF16), row(b_pw), ts=TOK_TILE)
    at = _attn(q.reshape(b, s, ATTN_W), k.reshape(b, s, ATTN_W), vt,
               km.reshape(b, s // MOBA_BLOCK, ATTN_W), _alibi_slopes(ATTN_HEADS), heads=ATTN_HEADS_PER_STEP)
    x1, h2, qp = _merge(at.reshape(t, ATTN_W), cm.reshape(t, d), ga, x2, w_ao.astype(BF16), w_o.astype(BF16),
                        row(g2), w_q.astype(BF16), tm=TOK_TILE)
    r2, e2, l1, c1 = _peer_topk(qp, keys1, keys2, tt=TOPK_TOK_TILE)
    return _peer_dense(h2, u_tab.astype(BF16), v_tab.astype(BF16).T, r2, e2, l1, c1, x1, gf,
                       tt=PEER_TOK_TILE, w=PEER_EXP_TILE)


def kernel(x, g_norm1, w_in, conv_w, conv_b, conv_ln_g, conv_ln_b, w_conv_pw, b_conv_pw, w_attn_out, w_out,
           g_norm2, w_peer_q, peer_keys1, peer_keys2, peer_u, peer_v, g_final):
    b, s, d = x.shape
    depth = w_in.shape[0]
    assert depth == 1, "the final RMSNorm is fused into the last layer's PEER kernel"
    assert s % TOK_TILE == 0 and d % LANES == 0
    x2 = x.reshape(b * s, d)
    l = 0
    out = _layer(x2, b, s, g_norm1[l], w_in[l], conv_w[l], conv_b[l], conv_ln_g[l], conv_ln_b[l], w_conv_pw[l],
                 b_conv_pw[l], w_attn_out[l], w_out[l], g_norm2[l], w_peer_q[l], peer_keys1[l], peer_keys2[l],
                 peer_u[l], peer_v[l], g_final.reshape(1, -1).astype(F32))
    return out.reshape(b, s, d)
```

```python
import functools
import math

import jax
import jax.numpy as jnp
from jax import lax
from jax.experimental import pallas as pl
from jax.experimental.pallas import tpu as pltpu

F32 = jnp.float32
BF16 = jnp.bfloat16

CONV_CH = 512
CONV_WIDTH = 31
ATTN_HEADS = 8
HEAD_DIM = 64
ATTN_W = ATTN_HEADS * HEAD_DIM
MOBA_BLOCK = 256
MOBA_TOPK = 3
PEER_HEADS = 8
PEER_NKEYS = 128
PEER_QDIM = 256
PEER_TOPK = 16
NORM_EPS = 1e-6

LANES = 128
SUBLANES = 8
CONV_HALO = 32
VMEM_LIMIT = 56 * 1024 * 1024

TOK_TILE = 512
TOPK_TOK_TILE = 256
PEER_TOK_TILE = 512
PEER_EXP_TILE = 2048
ATTN_KV_GROUP = 2
ATTN_HEADS_PER_STEP = 8
ATTN_STAGE_HEADS = 8

NEG_INF = float("-inf")
NOT_RANKED = 99.0


def _params(sem):
    return pltpu.CompilerParams(dimension_semantics=sem, vmem_limit_bytes=VMEM_LIMIT)


def _nt_dot(a, b):
    return lax.dot_general(a, b, (((1,), (1,)), ((), ())), preferred_element_type=F32)


def _split_bf16(x):
    hi = x.astype(BF16)
    lo = (x - hi.astype(F32)).astype(BF16)
    return hi, lo


def _inproj_kernel(x_ref, g_ref, w_ref, u_ref, q_ref, k_ref, vt_ref, gc_ref, ga_ref, km_ref, *, d_model):
    x = x_ref[...]
    h = x * lax.rsqrt(jnp.mean(x * x, axis=-1, keepdims=True) + NORM_EPS) * g_ref[...]
    hb = h.astype(BF16)

    def proj(a, b):
        return jnp.dot(hb, w_ref[:, a:b], preferred_element_type=F32)

    c0 = 2 * CONV_CH
    u_ref[...] = proj(0, c0).astype(BF16)
    q_ref[...] = (proj(c0, c0 + ATTN_W) * (HEAD_DIM ** -0.5)).astype(BF16)
    kf = proj(c0 + ATTN_W, c0 + 2 * ATTN_W)
    k_ref[...] = kf.astype(BF16)
    vt_ref[0] = proj(c0 + 2 * ATTN_W, c0 + 3 * ATTN_W).T.astype(BF16)
    gc_ref[...] = proj(c0 + 3 * ATTN_W, c0 + 3 * ATTN_W + d_model).astype(BF16)
    ga_ref[...] = proj(c0 + 3 * ATTN_W + d_model, c0 + 3 * ATTN_W + 2 * d_model).astype(BF16)
    for blk in range(kf.shape[0] // MOBA_BLOCK):
        km_ref[blk] = jnp.mean(kf[blk * MOBA_BLOCK:(blk + 1) * MOBA_BLOCK], axis=0, keepdims=True)


def _inproj(x2, g1, w_in_b, tm, seq):
    t, d = x2.shape
    n_in = w_in_b.shape[1]
    tiles_per_seq = seq // tm
    tok = lambda w: pl.BlockSpec((tm, w), lambda i: (i, 0))
    const = lambda shape: pl.BlockSpec(shape, lambda i: (0,) * len(shape))
    nbt = tm // MOBA_BLOCK
    return pl.pallas_call(
        functools.partial(_inproj_kernel, d_model=d),
        grid=(t // tm,),
        in_specs=[tok(d), const((1, d)), const((d, n_in))],
        out_specs=[tok(2 * CONV_CH), tok(ATTN_W), tok(ATTN_W),
                   pl.BlockSpec((1, ATTN_W, tm), lambda i: (i // tiles_per_seq, 0, i % tiles_per_seq)),
                   tok(d), tok(d),
                   pl.BlockSpec((nbt, 1, ATTN_W), lambda i: (i, 0, 0))],
        out_shape=[jax.ShapeDtypeStruct((t, 2 * CONV_CH), BF16),
                   jax.ShapeDtypeStruct((t, ATTN_W), BF16),
                   jax.ShapeDtypeStruct((t, ATTN_W), BF16),
                   jax.ShapeDtypeStruct((t // seq, ATTN_W, seq), BF16),
                   jax.ShapeDtypeStruct((t, d), BF16),
                   jax.ShapeDtypeStruct((t, d), BF16),
                   jax.ShapeDtypeStruct((t // MOBA_BLOCK, 1, ATTN_W), F32)],
        compiler_params=_params(("parallel",)),
    )(x2, g1, w_in_b)


def _conv_kernel(u_ref, halo_ref, gc_ref, cw_ref, cb_ref, lg_ref, lb_ref, wpw_ref, bpw_ref, o_ref, zext_ref, zs_ref,
                 *, ts):
    def glu(u):
        u = u.astype(F32)
        return u[:, :CONV_CH] * jax.nn.sigmoid(u[:, CONV_CH:])

    zh = glu(halo_ref[0])
    zh = jnp.where(pl.program_id(1) == 0, jnp.zeros_like(zh), zh)
    zext_ref[0:CONV_HALO, :] = zh
    zext_ref[CONV_HALO:CONV_HALO + ts, :] = glu(u_ref[0])
    span = zs_ref.shape[1]
    for r in range(1, SUBLANES):
        zs_ref[r - 1] = zext_ref[r:r + span, :]
    acc = jnp.zeros((ts, CONV_CH), F32) + cb_ref[...]
    base = CONV_HALO - (CONV_WIDTH - 1)
    for w in range(CONV_WIDTH):
        r = (base + w) % SUBLANES
        q = base + w - r
        tap = zext_ref[q:q + ts, :] if r == 0 else zs_ref[r - 1, q:q + ts, :]
        acc = acc + tap * cw_ref[w:w + 1, :]
    mu = jnp.mean(acc, axis=-1, keepdims=True)
    xc = acc - mu
    var = jnp.mean(xc * xc, axis=-1, keepdims=True)
    y = xc * lax.rsqrt(var + NORM_EPS) * lg_ref[...] + lb_ref[...]
    y = y * jax.nn.sigmoid(y)
    yc = jnp.dot(y.astype(BF16), wpw_ref[...], preferred_element_type=F32) + bpw_ref[...]
    o_ref[0] = (jax.nn.sigmoid(gc_ref[0].astype(F32)) * yc).astype(BF16)


def _conv(u3, gc3, conv_w, conv_b, ln_g, ln_b, wpw_b, bpw, ts):
    b, s, d2 = u3.shape
    d = gc3.shape[-1]
    hpb = ts // CONV_HALO
    const = lambda shape: pl.BlockSpec(shape, lambda bi, si: (0,) * len(shape))
    return pl.pallas_call(
        functools.partial(_conv_kernel, ts=ts),
        grid=(b, s // ts),
        in_specs=[pl.BlockSpec((1, ts, d2), lambda bi, si: (bi, si, 0)),
                  pl.BlockSpec((1, CONV_HALO, d2), lambda bi, si: (bi, jnp.maximum(si * hpb - 1, 0), 0)),
                  pl.BlockSpec((1, ts, d), lambda bi, si: (bi, si, 0)),
                  const((CONV_WIDTH, CONV_CH)), const((1, CONV_CH)), const((1, CONV_CH)), const((1, CONV_CH)),
                  const((CONV_CH, d)), const((1, d))],
        out_specs=pl.BlockSpec((1, ts, d), lambda bi, si: (bi, si, 0)),
        out_shape=jax.ShapeDtypeStruct((b, s, d), BF16),
        scratch_shapes=[pltpu.VMEM((CONV_HALO + ts, CONV_CH), F32),
                        pltpu.VMEM((SUBLANES - 1, CONV_HALO + ts - SUBLANES, CONV_CH), F32)],
        compiler_params=_params(("parallel", "parallel")),
    )(u3, u3, gc3, conv_w, conv_b, ln_g, ln_b, wpw_b, bpw)


def _attn_kernel(slopes_ref, q_ref, k_ref, vt_ref, km_ref, o_ref, sel_ref, *, nb, heads):
    blk = MOBA_BLOCK
    hg = pl.program_id(1)
    qb = pl.program_id(2)
    heads_per_vreg = LANES // HEAD_DIM
    lane = lax.broadcasted_iota(jnp.int32, (blk, LANES), 1)
    kk = lax.broadcasted_iota(jnp.int32, (blk, blk), 0)
    qq = lax.broadcasted_iota(jnp.int32, (blk, blk), 1)
    biota = lax.broadcasted_iota(jnp.int32, (nb, blk), 0)
    own = pl.ds(pl.multiple_of(qb * blk, blk), blk)
    k_own = k_ref[0, own, :]
    vt_own = vt_ref[0, :, own]
    lanes_of = lambda h: slice((h // heads_per_vreg) * LANES, (h // heads_per_vreg + 1) * LANES)
    rows_of = lambda h: slice(h * HEAD_DIM, (h + 1) * HEAD_DIM)
    hs = range(heads)
    slopes = [slopes_ref[hg * heads + h] for h in hs]
    qhs = []
    for h in hs:
        h2 = h % heads_per_vreg
        hmask = (lane >= HEAD_DIM * h2) & (lane < HEAD_DIM * (h2 + 1))
        q2 = q_ref[0, :, lanes_of(h)]
        qhs.append(jnp.where(hmask, q2, jnp.zeros_like(q2)))
    grp = ATTN_KV_GROUP
    krow = lax.broadcasted_iota(jnp.int32, (grp * blk, LANES), 0) % blk
    klane = lax.broadcasted_iota(jnp.int32, (grp * blk, LANES), 1)
    kfeat = jnp.where(klane == 0, krow.astype(F32), 0.0).astype(BF16)
    q_aug = [jnp.concatenate([qhs[h], jnp.where(lane == 0, slopes[h], 0.0).astype(BF16)], axis=1) for h in hs]

    kms = [_split_bf16(km_ref[0, :, lanes_of(h)]) for h in hs]
    bss = [_nt_dot(kms[h][0], qhs[h]) + _nt_dot(kms[h][1], qhs[h]) for h in hs]
    bss = [jnp.where(biota < qb, bs, NEG_INF) for bs in bss]
    sels = [jnp.zeros((nb, blk), F32) for _ in hs]
    for _ in range(MOBA_TOPK):
        ms = [jnp.max(bs, axis=0, keepdims=True) for bs in bss]
        idxs = [jnp.min(jnp.where(bs == m, biota, nb), axis=0, keepdims=True) for bs, m in zip(bss, ms)]
        hits = [biota == idx for idx in idxs]
        sels = [jnp.where(hit & (m > NEG_INF), 1.0, sel) for hit, m, sel in zip(hits, ms, sels)]
        bss = [jnp.where(hit, NEG_INF, bs) for hit, bs in zip(hits, bss)]

    ss = [_nt_dot(jnp.concatenate([k_own[:, lanes_of(h)], kfeat[:blk]], axis=1), q_aug[h]) for h in hs]
    ss = [jnp.where(kk <= qq, s, NEG_INF) for s in ss]
    m0s = [jnp.max(s, axis=0, keepdims=True) for s in ss]
    ps = [jnp.exp(s - m0) for s, m0 in zip(ss, m0s)]
    l0s = [jnp.sum(p, axis=0, keepdims=True) for p in ps]
    acc0s = [jnp.dot(vt_own[rows_of(h), :], ps[h].astype(BF16), preferred_element_type=F32) for h in hs]
    for h in hs:
        sel_ref[h] = jnp.where(sels[h] > 0.5, 0.0, NEG_INF)

    def body(it, carry):
        ms, ls, accs = carry
        j0 = it * grp
        off = pl.multiple_of(j0 * blk, grp * blk)
        kj = k_ref[0, pl.ds(off, grp * blk), :]
        vtj = vt_ref[0, :, pl.ds(off, grp * blk)]
        mrows = [[sel_ref[h, pl.ds(j0 + g, 1), :] - slopes[h] * ((qb - j0 - g) * blk).astype(F32)
                  for g in range(grp)] for h in hs]
        m_new, l_new, acc_new = [None] * heads, [None] * heads, [None] * heads
        for h0 in range(0, heads, ATTN_STAGE_HEADS):
            hb = range(h0, h0 + ATTN_STAGE_HEADS)
            sfull = {h: _nt_dot(jnp.concatenate([kj[:, lanes_of(h)], kfeat], axis=1), q_aug[h])
                     for h in hb}
            ss = {h: [sfull[h][g * blk:(g + 1) * blk, :] for g in range(grp)] for h in hb}
            for h in hb:
                m_new[h] = functools.reduce(
                    jnp.maximum,
                    [ms[h]] + [jnp.max(ss[h][g], axis=0, keepdims=True) + mrows[h][g] for g in range(grp)])
            alpha = {h: jnp.exp(ms[h] - m_new[h]) for h in hb}
            ps = {h: [jnp.exp(ss[h][g] - (m_new[h] - mrows[h][g])) for g in range(grp)] for h in hb}
            psum = {h: functools.reduce(lambda a, b_: a + b_, [jnp.sum(p, axis=0, keepdims=True) for p in ps[h]])
                    for h in hb}
            for h in hb:
                l_new[h] = alpha[h] * ls[h] + psum[h]
            pcat = {h: jnp.concatenate([p.astype(BF16) for p in ps[h]], axis=0) for h in hb}
            pv = {h: jnp.dot(vtj[rows_of(h), :], pcat[h], preferred_element_type=F32) for h in hb}
            for h in hb:
                acc_new[h] = alpha[h] * accs[h] + pv[h]
        return tuple(m_new), tuple(l_new), tuple(acc_new)

    _, l_fin, acc_fin = lax.fori_loop(0, (qb + grp - 1) // grp, body, (tuple(m0s), tuple(l0s), tuple(acc0s)))
    ot = jnp.concatenate([acc_fin[h] / l_fin[h] for h in hs], axis=0)
    o_ref[0] = ot.T.astype(BF16)


def _attn(q3, k3, vt3, km3, slopes, heads):
    b, s, w = q3.shape
    nb = s // MOBA_BLOCK
    hw = heads * HEAD_DIM
    assert nb % ATTN_KV_GROUP == 0 and hw % LANES == 0 and w % hw == 0
    return pl.pallas_call(
        functools.partial(_attn_kernel, nb=nb, heads=heads),
        grid=(b, w // hw, nb),
        in_specs=[pl.BlockSpec(memory_space=pltpu.SMEM),
                  pl.BlockSpec((1, MOBA_BLOCK, hw), lambda bi, hg, qb: (bi, qb, hg)),
                  pl.BlockSpec((1, s, hw), lambda bi, hg, qb: (bi, 0, hg)),
                  pl.BlockSpec((1, hw, s), lambda bi, hg, qb: (bi, hg, 0)),
                  pl.BlockSpec((1, nb, hw), lambda bi, hg, qb: (bi, 0, hg))],
        out_specs=pl.BlockSpec((1, MOBA_BLOCK, hw), lambda bi, hg, qb: (bi, qb, hg)),
        out_shape=jax.ShapeDtypeStruct((b, s, w), BF16),
        scratch_shapes=[pltpu.VMEM((heads, nb, MOBA_BLOCK), F32)],
        compiler_params=_params(("parallel", "parallel", "arbitrary")),
    )(slopes, q3, k3, vt3, km3)


def _merge_kernel(at_ref, cm_ref, ga_ref, x_ref, wao_ref, wo_ref, g2_ref, wq_ref, x1_ref, h2_ref, qp_ref):
    ya = jnp.dot(at_ref[...], wao_ref[...], preferred_element_type=F32)
    mix = cm_ref[...].astype(F32) + jax.nn.sigmoid(ga_ref[...].astype(F32)) * ya
    x1 = x_ref[...] + jnp.dot(mix.astype(BF16), wo_ref[...], preferred_element_type=F32)
    x1_ref[...] = x1
    h2 = x1 * lax.rsqrt(jnp.mean(x1 * x1, axis=-1, keepdims=True) + NORM_EPS) * g2_ref[...]
    h2b = h2.astype(BF16)
    h2_ref[...] = h2b
    qp_ref[...] = jnp.dot(h2b, wq_ref[...], preferred_element_type=F32)


def _merge(at2, cm2, ga2, x2, wao_b, wo_b, g2, wq_b, tm):
    t, d = x2.shape
    nq = wq_b.shape[1]
    tok = lambda w: pl.BlockSpec((tm, w), lambda i: (i, 0))
    const = lambda shape: pl.BlockSpec(shape, lambda i: (0,) * len(shape))
    return pl.pallas_call(
        _merge_kernel,
        grid=(t // tm,),
        in_specs=[tok(ATTN_W), tok(d), tok(d), tok(d),
                  const((ATTN_W, d)), const((d, d)), const((1, d)), const((d, nq))],
        out_specs=[tok(d), tok(d), tok(nq)],
        out_shape=[jax.ShapeDtypeStruct((t, d), F32),
                   jax.ShapeDtypeStruct((t, d), BF16),
                   jax.ShapeDtypeStruct((t, nq), F32)],
        compiler_params=_params(("parallel",)),
    )(at2, cm2, ga2, x2, wao_b, wo_b, g2, wq_b)


def _top_ranked(x, kiota, exact_ties):
    rank = jnp.full(x.shape, NOT_RANKED, F32)
    vals = []
    for r in range(PEER_TOPK):
        m = jnp.max(x, axis=0, keepdims=True)
        hit = x == m
        if exact_ties:
            idx = jnp.min(jnp.where(hit, kiota, float(x.shape[0])), axis=0, keepdims=True)
            hit = kiota == idx
        x = jnp.where(hit, NEG_INF, x)
        rank = jnp.where(hit, float(r), rank)
        vals.append(m)
    return rank, vals


def _peer_select(s1, s2, kiota, exact_ties):
    k = PEER_TOPK
    g = SUBLANES
    tt = s1.shape[1]
    rank1, v1 = _top_ranked(s1, kiota, exact_ties)
    rank2, v2 = _top_ranked(s2, kiota, exact_ties)

    sub = lax.broadcasted_iota(jnp.int32, (g, tt), 0).astype(F32)
    v2a = jnp.concatenate(v2[:g], axis=0)
    v2b = jnp.concatenate(v2[g:], axis=0)
    v1b = jnp.concatenate(v1[g:], axis=0)
    cands = [v1[0] + v2a, v1[0] + v2b] + [v1[r] + v2a for r in range(1, g)] + [v1b + v2[0]]
    poss = [sub, sub + float(g)] + [sub + float(r * k) for r in range(1, g)] + [(sub + float(g)) * float(k)]
    orig = list(cands)
    picked = [jnp.zeros((g, tt), F32) for _ in cands]
    for _ in range(k):
        m = functools.reduce(jnp.maximum, cands)
        m = jnp.max(m, axis=0, keepdims=True)
        hits = [c == m for c in cands]
        if exact_ties:
            pm = functools.reduce(jnp.minimum, [jnp.where(h, p, 1e9) for h, p in zip(hits, poss)])
            pm = jnp.min(pm, axis=0, keepdims=True)
            hits = [p == pm for p in poss]
        cands = [jnp.where(h, NEG_INF, c) for h, c in zip(hits, cands)]
        picked = [jnp.where(h, 1.0, s) for h, s in zip(hits, picked)]
    cmax = v1[0] + v2[0]
    z = functools.reduce(
        lambda a, b_: a + b_,
        [jnp.sum(jnp.where(s > 0.5, jnp.exp(c - cmax), 0.0), axis=0, keepdims=True) for s, c in zip(picked, orig)])
    lrow = [jnp.sum(picked[0] + picked[1], axis=0, keepdims=True)]
    lrow += [jnp.sum(picked[r + 1], axis=0, keepdims=True) for r in range(1, g)]
    lvec = jnp.concatenate(lrow + [picked[g + 1]], axis=0)

    l1 = jnp.zeros(s1.shape, F32)
    for r in range(k):
        l1 = jnp.where(rank1 == float(r), lvec[r:r + 1, :], l1)
    ranked = lambda rk: jnp.sum(jnp.where(rk < float(k), 1.0, 0.0), axis=0, keepdims=True)
    counts = jnp.concatenate([ranked(rank1), ranked(rank2), jnp.sum(lvec, axis=0, keepdims=True)], axis=0)
    e2 = jnp.exp(s2 - v2[0])
    c1 = jnp.exp(s1 - v1[0]) * (0.5 / z)
    return rank2, e2, l1, c1, counts


def _peer_topk_kernel(qp_ref, k1_ref, k2_ref, r2_ref, e2_ref, l1_ref, c1_ref):
    half = PEER_QDIM // 2
    tt = qp_ref.shape[0]
    qp = qp_ref[...]
    kiota = lax.broadcasted_iota(jnp.int32, (PEER_NKEYS, tt), 0).astype(F32)

    def scores(keys, qside):
        kh, kl = _split_bf16(keys)
        qh, ql = _split_bf16(qside)
        return _nt_dot(kh, qh) + (_nt_dot(kh, ql) + _nt_dot(kl, qh))

    s1 = scores(k1_ref[...], qp[:, :half])
    s2 = scores(k2_ref[...], qp[:, half:])
    fast = _peer_select(s1, s2, kiota, exact_ties=False)
    tied = jnp.max(jnp.abs(fast[4] - float(PEER_TOPK))) > 0.0
    rank2, e2, l1, c1, _ = lax.cond(tied, lambda: _peer_select(s1, s2, kiota, exact_ties=True), lambda: fast)
    r2_ref[0] = rank2.astype(BF16)
    e2_ref[0] = e2.astype(BF16)
    l1_ref[0] = l1
    c1_ref[0] = c1


def _peer_topk(qp, keys1, keys2, tt):
    t = qp.shape[0]
    shape = (PEER_HEADS, PEER_NKEYS, t)
    ospec = pl.BlockSpec((1, PEER_NKEYS, tt), lambda i, h: (h, 0, i))
    kspec = pl.BlockSpec(keys1.shape, lambda i, h: (0, 0))
    return pl.pallas_call(
        _peer_topk_kernel,
        grid=(t // tt, PEER_HEADS),
        in_specs=[pl.BlockSpec((tt, PEER_QDIM), lambda i, h: (i, h)), kspec, kspec],
        out_specs=[ospec] * 4,
        out_shape=[jax.ShapeDtypeStruct(shape, BF16), jax.ShapeDtypeStruct(shape, BF16),
                   jax.ShapeDtypeStruct(shape, F32), jax.ShapeDtypeStruct(shape, F32)],
        compiler_params=_params(("parallel", "parallel")),
    )(qp, keys1, keys2)


def _peer_gate(r2_ref, e2_ref, l1_ref, c1_ref, n1_per_tile):
    tt = r2_ref.shape[2]
    gates = []
    for a in range(n1_per_tile):
        gate = None
        for h in range(PEER_HEADS):
            cnt = jnp.broadcast_to(l1_ref[h, a:a + 1, :], (PEER_NKEYS, tt)).astype(BF16)
            c1 = jnp.broadcast_to(c1_ref[h, a:a + 1, :], (PEER_NKEYS, tt)).astype(BF16)
            e2 = e2_ref[h]
            term = jnp.where(r2_ref[h] < cnt, e2, jnp.zeros_like(e2)) * c1
            gate = term if gate is None else gate + term
        gates.append(gate)
    return gates


def _peer_dense_kernel(h2_ref, u_ref, vt_ref, r2_ref, e2_ref, l1_ref, c1_ref, x1_ref, gf_ref, o_ref,
                       acc_ref, *, n1_per_tile):
    j = pl.program_id(1)

    @pl.when(j == 0)
    def _():
        acc_ref[...] = jnp.zeros_like(acc_ref)

    gates = _peer_gate(r2_ref, e2_ref, l1_ref, c1_ref, n1_per_tile)
    at = _nt_dot(u_ref[...], h2_ref[...])
    parts = []
    for a in range(n1_per_tile):
        x = at[a * PEER_NKEYS:(a + 1) * PEER_NKEYS, :]
        act = x * (1.0 + lax.erf(x * math.sqrt(0.5)))
        parts.append(gates[a] * act.astype(BF16))
    ga = jnp.concatenate(parts, axis=0)
    acc_ref[...] += jnp.dot(vt_ref[...], ga, preferred_element_type=F32)

    @pl.when(j == pl.num_programs(1) - 1)
    def _():
        x2 = x1_ref[...] + acc_ref[...].T
        o_ref[...] = x2 * lax.rsqrt(jnp.mean(x2 * x2, axis=-1, keepdims=True) + NORM_EPS) * gf_ref[...]


def _peer_dense(h2, u_b, vt_b, r2, e2, l1, c1, x1, gf, tt, w):
    t, d = h2.shape
    ne = u_b.shape[0]
    n1_per_tile = w // PEER_NKEYS
    key_spec = pl.BlockSpec((PEER_HEADS, PEER_NKEYS, tt), lambda i, j: (0, 0, i))
    n1_spec = pl.BlockSpec((PEER_HEADS, n1_per_tile, tt), lambda i, j: (0, j, i))
    return pl.pallas_call(
        functools.partial(_peer_dense_kernel, n1_per_tile=n1_per_tile),
        grid=(t // tt, ne // w),
        in_specs=[pl.BlockSpec((tt, d), lambda i, j: (i, 0)),
                  pl.BlockSpec((w, d), lambda i, j: (j, 0)),
                  pl.BlockSpec((d, w), lambda i, j: (0, j)),
                  key_spec, key_spec, n1_spec, n1_spec,
                  pl.BlockSpec((tt, d), lambda i, j: (i, 0)),
                  pl.BlockSpec((1, d), lambda i, j: (0, 0))],
        out_specs=pl.BlockSpec((tt, d), lambda i, j: (i, 0)),
        out_shape=jax.ShapeDtypeStruct((t, d), F32),
        scratch_shapes=[pltpu.VMEM((d, tt), F32)],
        compiler_params=_params(("parallel", "arbitrary")),
    )(h2, u_b, vt_b, r2, e2, l1, c1, x1, gf)


def _alibi_slopes(n_heads):
    return jnp.asarray([2.0 ** (-8.0 * (i + 1) / n_heads) for i in range(n_heads)], dtype=F32)


def _layer(x2, b, s, g1, w_in, conv_w, conv_b, ln_g, ln_b, w_pw, b_pw, w_ao, w_o, g2, w_q, keys1, keys2, u_tab, v_tab, gf):
    t, d = x2.shape
    row = lambda a: a.reshape(1, -1).astype(F32)
    u, q, k, vt, gc, ga, km = _inproj(x2, row(g1), w_in.astype(BF16), tm=TOK_TILE, seq=s)
    cm = _conv(u.reshape(b, s, -1), gc.reshape(b, s, d), conv_w, row(conv_b), row(ln_g), row(ln_b),
               w_pw.astype(BF16), row(b_pw), ts=TOK_TILE)
    at = _attn(q.reshape(b, s, ATTN_W), k.reshape(b, s, ATTN_W), vt,
               km.reshape(b, s // MOBA_BLOCK, ATTN_W), _alibi_slopes(ATTN_HEADS), heads=ATTN_HEADS_PER_STEP)
    x1, h2, qp = _merge(at.reshape(t, ATTN_W), cm.reshape(t, d), ga, x2, w_ao.astype(BF16), w_o.astype(BF16),
                        row(g2), w_q.astype(BF16), tm=TOK_TILE)
    r2, e2, l1, c1 = _peer_topk(qp, keys1, keys2, tt=TOPK_TOK_TILE)
    return _peer_dense(h2, u_tab.astype(BF16), v_tab.astype(BF16).T, r2, e2, l1, c1, x1, gf,
                       tt=PEER_TOK_TILE, w=PEER_EXP_TILE)


def kernel(x, g_norm1, w_in, conv_w, conv_b, conv_ln_g, conv_ln_b, w_conv_pw, b_conv_pw, w_attn_out, w_out,
           g_norm2, w_peer_q, peer_keys1, peer_keys2, peer_u, peer_v, g_final):
    b, s, d = x.shape
    depth = w_in.shape[0]
    assert depth == 1, "the final RMSNorm is fused into the last layer's PEER kernel"
    assert s % TOK_TILE == 0 and d % LANES == 0
    x2 = x.reshape(b * s, d)
    l = 0
    out = _layer(x2, b, s, g_norm1[l], w_in[l], conv_w[l], conv_b[l], conv_ln_g[l], conv_ln_b[l], w_conv_pw[l],
                 b_conv_pw[l], w_attn_out[l], w_out[l], g_norm2[l], w_peer_q[l], peer_keys1[l], peer_keys2[l],
                 peer_u[l], peer_v[l], g_final.reshape(1, -1).astype(F32))
    return out.reshape(b, s, d)
```

```python
import functools
import math

import jax
import jax.numpy as jnp
from jax import lax
from jax.experimental import pallas as pl
from jax.experimental.pallas import tpu as pltpu

F32 = jnp.float32
BF16 = jnp.bfloat16

CONV_CH = 512
CONV_WIDTH = 31
ATTN_HEADS = 8
HEAD_DIM = 64
ATTN_W = ATTN_HEADS * HEAD_DIM
MOBA_BLOCK = 256
MOBA_TOPK = 3
PEER_HEADS = 8
PEER_NKEYS = 128
PEER_QDIM = 256
PEER_TOPK = 16
NORM_EPS = 1e-6

LANES = 128
SUBLANES = 8
CONV_HALO = 32
VMEM_LIMIT = 56 * 1024 * 1024

TOK_TILE = 512
TOPK_TOK_TILE = 256
PEER_TOK_TILE = 512
PEER_EXP_TILE = 2048
ATTN_KV_GROUP = 4
ATTN_KV_TAIL = 2
ATTN_HEADS_PER_STEP = 8
ATTN_STAGE_HEADS = 8

NEG_INF = float("-inf")
NOT_RANKED = 99.0


def _params(sem):
    return pltpu.CompilerParams(dimension_semantics=sem, vmem_limit_bytes=VMEM_LIMIT)


def _nt_dot(a, b):
    return lax.dot_general(a, b, (((1,), (1,)), ((), ())), preferred_element_type=F32)


def _split_bf16(x):
    hi = x.astype(BF16)
    lo = (x - hi.astype(F32)).astype(BF16)
    return hi, lo


def _inproj_kernel(x_ref, g_ref, w_ref, u_ref, q_ref, k_ref, vt_ref, gc_ref, ga_ref, km_ref, *, d_model):
    x = x_ref[...]
    h = x * lax.rsqrt(jnp.mean(x * x, axis=-1, keepdims=True) + NORM_EPS) * g_ref[...]
    hb = h.astype(BF16)

    def proj(a, b):
        return jnp.dot(hb, w_ref[:, a:b], preferred_element_type=F32)

    c0 = 2 * CONV_CH
    u_ref[...] = proj(0, c0).astype(BF16)
    q_ref[...] = (proj(c0, c0 + ATTN_W) * (HEAD_DIM ** -0.5)).astype(BF16)
    kf = proj(c0 + ATTN_W, c0 + 2 * ATTN_W)
    k_ref[...] = kf.astype(BF16)
    vt_ref[0] = proj(c0 + 2 * ATTN_W, c0 + 3 * ATTN_W).T.astype(BF16)
    gc_ref[...] = proj(c0 + 3 * ATTN_W, c0 + 3 * ATTN_W + d_model).astype(BF16)
    ga_ref[...] = proj(c0 + 3 * ATTN_W + d_model, c0 + 3 * ATTN_W + 2 * d_model).astype(BF16)
    for blk in range(kf.shape[0] // MOBA_BLOCK):
        km_ref[blk] = jnp.mean(kf[blk * MOBA_BLOCK:(blk + 1) * MOBA_BLOCK], axis=0, keepdims=True)


def _inproj(x2, g1, w_in_b, tm, seq):
    t, d = x2.shape
    n_in = w_in_b.shape[1]
    tiles_per_seq = seq // tm
    tok = lambda w: pl.BlockSpec((tm, w), lambda i: (i, 0))
    const = lambda shape: pl.BlockSpec(shape, lambda i: (0,) * len(shape))
    nbt = tm // MOBA_BLOCK
    return pl.pallas_call(
        functools.partial(_inproj_kernel, d_model=d),
        grid=(t // tm,),
        in_specs=[tok(d), const((1, d)), const((d, n_in))],
        out_specs=[tok(2 * CONV_CH), tok(ATTN_W), tok(ATTN_W),
                   pl.BlockSpec((1, ATTN_W, tm), lambda i: (i // tiles_per_seq, 0, i % tiles_per_seq)),
                   tok(d), tok(d),
                   pl.BlockSpec((nbt, 1, ATTN_W), lambda i: (i, 0, 0))],
        out_shape=[jax.ShapeDtypeStruct((t, 2 * CONV_CH), BF16),
                   jax.ShapeDtypeStruct((t, ATTN_W), BF16),
                   jax.ShapeDtypeStruct((t, ATTN_W), BF16),
                   jax.ShapeDtypeStruct((t // seq, ATTN_W, seq), BF16),
                   jax.ShapeDtypeStruct((t, d), BF16),
                   jax.ShapeDtypeStruct((t, d), BF16),
                   jax.ShapeDtypeStruct((t // MOBA_BLOCK, 1, ATTN_W), F32)],
        compiler_params=_params(("parallel",)),
    )(x2, g1, w_in_b)


def _conv_kernel(u_ref, halo_ref, gc_ref, cw_ref, cb_ref, lg_ref, lb_ref, wpw_ref, bpw_ref, o_ref, zext_ref, zs_ref,
                 *, ts):
    def glu(u):
        u = u.astype(F32)
        return u[:, :CONV_CH] * jax.nn.sigmoid(u[:, CONV_CH:])

    zh = glu(halo_ref[0])
    zh = jnp.where(pl.program_id(1) == 0, jnp.zeros_like(zh), zh)
    zext_ref[0:CONV_HALO, :] = zh
    zext_ref[CONV_HALO:CONV_HALO + ts, :] = glu(u_ref[0])
    span = zs_ref.shape[1]
    for r in range(1, SUBLANES):
        zs_ref[r - 1] = zext_ref[r:r + span, :]
    acc = jnp.zeros((ts, CONV_CH), F32) + cb_ref[...]
    base = CONV_HALO - (CONV_WIDTH - 1)
    for w in range(CONV_WIDTH):
        r = (base + w) % SUBLANES
        q = base + w - r
        tap = zext_ref[q:q + ts, :] if r == 0 else zs_ref[r - 1, q:q + ts, :]
        acc = acc + tap * cw_ref[w:w + 1, :]
    mu = jnp.mean(acc, axis=-1, keepdims=True)
    xc = acc - mu
    var = jnp.mean(xc * xc, axis=-1, keepdims=True)
    y = xc * lax.rsqrt(var + NORM_EPS) * lg_ref[...] + lb_ref[...]
    y = y * jax.nn.sigmoid(y)
    yc = jnp.dot(y.astype(BF16), wpw_ref[...], preferred_element_type=F32) + bpw_ref[...]
    o_ref[0] = (jax.nn.sigmoid(gc_ref[0].astype(F32)) * yc).astype(BF16)


def _conv(u3, gc3, conv_w, conv_b, ln_g, ln_b, wpw_b, bpw, ts):
    b, s, d2 = u3.shape
    d = gc3.shape[-1]
    hpb = ts // CONV_HALO
    const = lambda shape: pl.BlockSpec(shape, lambda bi, si: (0,) * len(shape))
    return pl.pallas_call(
        functools.partial(_conv_kernel, ts=ts),
        grid=(b, s // ts),
        in_specs=[pl.BlockSpec((1, ts, d2), lambda bi, si: (bi, si, 0)),
                  pl.BlockSpec((1, CONV_HALO, d2), lambda bi, si: (bi, jnp.maximum(si * hpb - 1, 0), 0)),
                  pl.BlockSpec((1, ts, d), lambda bi, si: (bi, si, 0)),
                  const((CONV_WIDTH, CONV_CH)), const((1, CONV_CH)), const((1, CONV_CH)), const((1, CONV_CH)),
                  const((CONV_CH, d)), const((1, d))],
        out_specs=pl.BlockSpec((1, ts, d), lambda bi, si: (bi, si, 0)),
        out_shape=jax.ShapeDtypeStruct((b, s, d), BF16),
        scratch_shapes=[pltpu.VMEM((CONV_HALO + ts, CONV_CH), F32),
                        pltpu.VMEM((SUBLANES - 1, CONV_HALO + ts - SUBLANES, CONV_CH), F32)],
        compiler_params=_params(("parallel", "parallel")),
    )(u3, u3, gc3, conv_w, conv_b, ln_g, ln_b, wpw_b, bpw)


def _attn_kernel(slopes_ref, q_ref, k_ref, vt_ref, km_ref, o_ref, sel_ref, *, nb, heads):
    blk = MOBA_BLOCK
    hg = pl.program_id(1)
    qb = pl.program_id(2)
    heads_per_vreg = LANES // HEAD_DIM
    lane = lax.broadcasted_iota(jnp.int32, (blk, LANES), 1)
    kk = lax.broadcasted_iota(jnp.int32, (blk, blk), 0)
    qq = lax.broadcasted_iota(jnp.int32, (blk, blk), 1)
    biota = lax.broadcasted_iota(jnp.int32, (nb, blk), 0)
    own = pl.ds(pl.multiple_of(qb * blk, blk), blk)
    k_own = k_ref[0, own, :]
    vt_own = vt_ref[0, :, own]
    lanes_of = lambda h: slice((h // heads_per_vreg) * LANES, (h // heads_per_vreg + 1) * LANES)
    rows_of = lambda h: slice(h * HEAD_DIM, (h + 1) * HEAD_DIM)
    hs = range(heads)
    slopes = [slopes_ref[hg * heads + h] for h in hs]
    qhs = []
    for h in hs:
        h2 = h % heads_per_vreg
        hmask = (lane >= HEAD_DIM * h2) & (lane < HEAD_DIM * (h2 + 1))
        q2 = q_ref[0, :, lanes_of(h)]
        qhs.append(jnp.where(hmask, q2, jnp.zeros_like(q2)))
    grp_max = max(ATTN_KV_GROUP, ATTN_KV_TAIL)
    krow = lax.broadcasted_iota(jnp.int32, (grp_max * blk, LANES), 0) % blk
    klane = lax.broadcasted_iota(jnp.int32, (grp_max * blk, LANES), 1)
    kfeat_max = jnp.where(klane == 0, krow.astype(F32), 0.0).astype(BF16)
    kfeat = kfeat_max
    q_aug = [jnp.concatenate([qhs[h], jnp.where(lane == 0, slopes[h], 0.0).astype(BF16)], axis=1) for h in hs]

    kms = [_split_bf16(km_ref[0, :, lanes_of(h)]) for h in hs]
    bss = [_nt_dot(kms[h][0], qhs[h]) + _nt_dot(kms[h][1], qhs[h]) for h in hs]
    bss = [jnp.where(biota < qb, bs, NEG_INF) for bs in bss]
    sels = [jnp.zeros((nb, blk), F32) for _ in hs]
    for _ in range(MOBA_TOPK):
        ms = [jnp.max(bs, axis=0, keepdims=True) for bs in bss]
        idxs = [jnp.min(jnp.where(bs == m, biota, nb), axis=0, keepdims=True) for bs, m in zip(bss, ms)]
        hits = [biota == idx for idx in idxs]
        sels = [jnp.where(hit & (m > NEG_INF), 1.0, sel) for hit, m, sel in zip(hits, ms, sels)]
        bss = [jnp.where(hit, NEG_INF, bs) for hit, bs in zip(hits, bss)]

    ss = [_nt_dot(jnp.concatenate([k_own[:, lanes_of(h)], kfeat[:blk]], axis=1), q_aug[h]) for h in hs]
    ss = [jnp.where(kk <= qq, s, NEG_INF) for s in ss]
    m0s = [jnp.max(s, axis=0, keepdims=True) for s in ss]
    ps = [jnp.exp(s - m0) for s, m0 in zip(ss, m0s)]
    l0s = [jnp.sum(p, axis=0, keepdims=True) for p in ps]
    acc0s = [jnp.dot(vt_own[rows_of(h), :], ps[h].astype(BF16), preferred_element_type=F32) for h in hs]
    for h in hs:
        sel_ref[h] = jnp.where(sels[h] > 0.5, 0.0, NEG_INF)

    def body(it, carry, grp, first):
        ms, ls, accs = carry
        j0 = first + it * grp
        kfeat = kfeat_max[:grp * blk]
        off = pl.multiple_of(j0 * blk, grp * blk)
        kj = k_ref[0, pl.ds(off, grp * blk), :]
        vtj = vt_ref[0, :, pl.ds(off, grp * blk)]
        mrows = [[sel_ref[h, pl.ds(j0 + g, 1), :] - slopes[h] * ((qb - j0 - g) * blk).astype(F32)
                  for g in range(grp)] for h in hs]
        m_new, l_new, acc_new = [None] * heads, [None] * heads, [None] * heads
        for h0 in range(0, heads, ATTN_STAGE_HEADS):
            hb = range(h0, h0 + ATTN_STAGE_HEADS)
            sfull = {h: _nt_dot(jnp.concatenate([kj[:, lanes_of(h)], kfeat], axis=1), q_aug[h])
                     for h in hb}
            ss = {h: [sfull[h][g * blk:(g + 1) * blk, :] for g in range(grp)] for h in hb}
            for h in hb:
                m_new[h] = functools.reduce(
                    jnp.maximum,
                    [ms[h]] + [jnp.max(ss[h][g], axis=0, keepdims=True) + mrows[h][g] for g in range(grp)])
            alpha = {h: jnp.exp(ms[h] - m_new[h]) for h in hb}
            ps = {h: [jnp.exp(ss[h][g] - (m_new[h] - mrows[h][g])) for g in range(grp)] for h in hb}
            psum = {h: functools.reduce(lambda a, b_: a + b_, [jnp.sum(p, axis=0, keepdims=True) for p in ps[h]])
                    for h in hb}
            for h in hb:
                l_new[h] = alpha[h] * ls[h] + psum[h]
            pcat = {h: jnp.concatenate([p.astype(BF16) for p in ps[h]], axis=0) for h in hb}
            pv = {h: jnp.dot(vtj[rows_of(h), :], pcat[h], preferred_element_type=F32) for h in hb}
            for h in hb:
                acc_new[h] = alpha[h] * accs[h] + pv[h]
        return tuple(m_new), tuple(l_new), tuple(acc_new)

    big, small = ATTN_KV_GROUP, ATTN_KV_TAIL
    n_big = (qb + 1) // big
    done = n_big * big
    carry = lax.fori_loop(0, n_big, functools.partial(body, grp=big, first=0), (tuple(m0s), tuple(l0s), tuple(acc0s)))
    _, l_fin, acc_fin = lax.fori_loop(0, (qb > done).astype(jnp.int32), functools.partial(body, grp=small, first=done),
                                      carry)
    ot = jnp.concatenate([acc_fin[h] / l_fin[h] for h in hs], axis=0)
    o_ref[0] = ot.T.astype(BF16)


def _attn(q3, k3, vt3, km3, slopes, heads):
    b, s, w = q3.shape
    nb = s // MOBA_BLOCK
    hw = heads * HEAD_DIM
    assert nb % ATTN_KV_GROUP == 0 and ATTN_KV_GROUP % ATTN_KV_TAIL == 0 and hw % LANES == 0 and w % hw == 0
    return pl.pallas_call(
        functools.partial(_attn_kernel, nb=nb, heads=heads),
        grid=(b, w // hw, nb),
        in_specs=[pl.BlockSpec(memory_space=pltpu.SMEM),
                  pl.BlockSpec((1, MOBA_BLOCK, hw), lambda bi, hg, qb: (bi, qb, hg)),
                  pl.BlockSpec((1, s, hw), lambda bi, hg, qb: (bi, 0, hg)),
                  pl.BlockSpec((1, hw, s), lambda bi, hg, qb: (bi, hg, 0)),
                  pl.BlockSpec((1, nb, hw), lambda bi, hg, qb: (bi, 0, hg))],
        out_specs=pl.BlockSpec((1, MOBA_BLOCK, hw), lambda bi, hg, qb: (bi, qb, hg)),
        out_shape=jax.ShapeDtypeStruct((b, s, w), BF16),
        scratch_shapes=[pltpu.VMEM((heads, nb, MOBA_BLOCK), F32)],
        compiler_params=_params(("parallel", "parallel", "arbitrary")),
    )(slopes, q3, k3, vt3, km3)


def _merge_kernel(at_ref, cm_ref, ga_ref, x_ref, wao_ref, wo_ref, g2_ref, wq_ref, x1_ref, h2_ref, qp_ref):
    ya = jnp.dot(at_ref[...], wao_ref[...], preferred_element_type=F32)
    mix = cm_ref[...].astype(F32) + jax.nn.sigmoid(ga_ref[...].astype(F32)) * ya
    x1 = x_ref[...] + jnp.dot(mix.astype(BF16), wo_ref[...], preferred_element_type=F32)
    x1_ref[...] = x1
    h2 = x1 * lax.rsqrt(jnp.mean(x1 * x1, axis=-1, keepdims=True) + NORM_EPS) * g2_ref[...]
    h2b = h2.astype(BF16)
    h2_ref[...] = h2b
    qp_ref[...] = jnp.dot(h2b, wq_ref[...], preferred_element_type=F32)


def _merge(at2, cm2, ga2, x2, wao_b, wo_b, g2, wq_b, tm):
    t, d = x2.shape
    nq = wq_b.shape[1]
    tok = lambda w: pl.BlockSpec((tm, w), lambda i: (i, 0))
    const = lambda shape: pl.BlockSpec(shape, lambda i: (0,) * len(shape))
    return pl.pallas_call(
        _merge_kernel,
        grid=(t // tm,),
        in_specs=[tok(ATTN_W), tok(d), tok(d), tok(d),
                  const((ATTN_W, d)), const((d, d)), const((1, d)), const((d, nq))],
        out_specs=[tok(d), tok(d), tok(nq)],
        out_shape=[jax.ShapeDtypeStruct((t, d), F32),
                   jax.ShapeDtypeStruct((t, d), BF16),
                   jax.ShapeDtypeStruct((t, nq), F32)],
        compiler_params=_params(("parallel",)),
    )(at2, cm2, ga2, x2, wao_b, wo_b, g2, wq_b)


def _top_ranked(x, kiota, exact_ties):
    rank = jnp.full(x.shape, NOT_RANKED, F32)
    vals = []
    for r in range(PEER_TOPK):
        m = jnp.max(x, axis=0, keepdims=True)
        hit = x == m
        if exact_ties:
            idx = jnp.min(jnp.where(hit, kiota, float(x.shape[0])), axis=0, keepdims=True)
            hit = kiota == idx
        x = jnp.where(hit, NEG_INF, x)
        rank = jnp.where(hit, float(r), rank)
        vals.append(m)
    return rank, vals


def _peer_select(s1, s2, kiota, exact_ties):
    k = PEER_TOPK
    g = SUBLANES
    tt = s1.shape[1]
    rank1, v1 = _top_ranked(s1, kiota, exact_ties)
    rank2, v2 = _top_ranked(s2, kiota, exact_ties)

    sub = lax.broadcasted_iota(jnp.int32, (g, tt), 0).astype(F32)
    v2a = jnp.concatenate(v2[:g], axis=0)
    v2b = jnp.concatenate(v2[g:], axis=0)
    v1b = jnp.concatenate(v1[g:], axis=0)
    cands = [v1[0] + v2a, v1[0] + v2b] + [v1[r] + v2a for r in range(1, g)] + [v1b + v2[0]]
    poss = [sub, sub + float(g)] + [sub + float(r * k) for r in range(1, g)] + [(sub + float(g)) * float(k)]
    orig = list(cands)
    picked = [jnp.zeros((g, tt), F32) for _ in cands]
    for _ in range(k):
        m = functools.reduce(jnp.maximum, cands)
        m = jnp.max(m, axis=0, keepdims=True)
        hits = [c == m for c in cands]
        if exact_ties:
            pm = functools.reduce(jnp.minimum, [jnp.where(h, p, 1e9) for h, p in zip(hits, poss)])
            pm = jnp.min(pm, axis=0, keepdims=True)
            hits = [p == pm for p in poss]
        cands = [jnp.where(h, NEG_INF, c) for h, c in zip(hits, cands)]
        picked = [jnp.where(h, 1.0, s) for h, s in zip(hits, picked)]
    cmax = v1[0] + v2[0]
    z = functools.reduce(
        lambda a, b_: a + b_,
        [jnp.sum(jnp.where(s > 0.5, jnp.exp(c - cmax), 0.0), axis=0, keepdims=True) for s, c in zip(picked, orig)])
    lrow = [jnp.sum(picked[0] + picked[1], axis=0, keepdims=True)]
    lrow += [jnp.sum(picked[r + 1], axis=0, keepdims=True) for r in range(1, g)]
    lvec = jnp.concatenate(lrow + [picked[g + 1]], axis=0)

    l1 = jnp.zeros(s1.shape, F32)
    for r in range(k):
        l1 = jnp.where(rank1 == float(r), lvec[r:r + 1, :], l1)
    ranked = lambda rk: jnp.sum(jnp.where(rk < float(k), 1.0, 0.0), axis=0, keepdims=True)
    counts = jnp.concatenate([ranked(rank1), ranked(rank2), jnp.sum(lvec, axis=0, keepdims=True)], axis=0)
    e2 = jnp.exp(s2 - v2[0])
    c1 = jnp.exp(s1 - v1[0]) * (0.5 / z)
    return rank2, e2, l1, c1, counts


def _peer_topk_kernel(qp_ref, k1_ref, k2_ref, r2_ref, e2_ref, l1_ref, c1_ref):
    half = PEER_QDIM // 2
    tt = qp_ref.shape[0]
    qp = qp_ref[...]
    kiota = lax.broadcasted_iota(jnp.int32, (PEER_NKEYS, tt), 0).astype(F32)

    def scores(keys, qside):
        kh, kl = _split_bf16(keys)
        qh, ql = _split_bf16(qside)
        return _nt_dot(kh, qh) + (_nt_dot(kh, ql) + _nt_dot(kl, qh))

    s1 = scores(k1_ref[...], qp[:, :half])
    s2 = scores(k2_ref[...], qp[:, half:])
    fast = _peer_select(s1, s2, kiota, exact_ties=False)
    tied = jnp.max(jnp.abs(fast[4] - float(PEER_TOPK))) > 0.0
    rank2, e2, l1, c1, _ = lax.cond(tied, lambda: _peer_select(s1, s2, kiota, exact_ties=True), lambda: fast)
    r2_ref[0] = rank2.astype(BF16)
    e2_ref[0] = e2.astype(BF16)
    l1_ref[0] = l1
    c1_ref[0] = c1


def _peer_topk(qp, keys1, keys2, tt):
    t = qp.shape[0]
    shape = (PEER_HEADS, PEER_NKEYS, t)
    ospec = pl.BlockSpec((1, PEER_NKEYS, tt), lambda i, h: (h, 0, i))
    kspec = pl.BlockSpec(keys1.shape, lambda i, h: (0, 0))
    return pl.pallas_call(
        _peer_topk_kernel,
        grid=(t // tt, PEER_HEADS),
        in_specs=[pl.BlockSpec((tt, PEER_QDIM), lambda i, h: (i, h)), kspec, kspec],
        out_specs=[ospec] * 4,
        out_shape=[jax.ShapeDtypeStruct(shape, BF16), jax.ShapeDtypeStruct(shape, BF16),
                   jax.ShapeDtypeStruct(shape, F32), jax.ShapeDtypeStruct(shape, F32)],
        compiler_params=_params(("parallel", "parallel")),
    )(qp, keys1, keys2)


def _peer_gate(r2_ref, e2_ref, l1_ref, c1_ref, n1_per_tile):
    tt = r2_ref.shape[2]
    gates = []
    for a in range(n1_per_tile):
        gate = None
        for h in range(PEER_HEADS):
            cnt = jnp.broadcast_to(l1_ref[h, a:a + 1, :], (PEER_NKEYS, tt)).astype(BF16)
            c1 = jnp.broadcast_to(c1_ref[h, a:a + 1, :], (PEER_NKEYS, tt)).astype(BF16)
            e2 = e2_ref[h]
            term = jnp.where(r2_ref[h] < cnt, e2, jnp.zeros_like(e2)) * c1
            gate = term if gate is None else gate + term
        gates.append(gate)
    return gates


def _peer_dense_kernel(h2_ref, u_ref, vt_ref, r2_ref, e2_ref, l1_ref, c1_ref, x1_ref, gf_ref, o_ref,
                       acc_ref, *, n1_per_tile):
    j = pl.program_id(1)

    @pl.when(j == 0)
    def _():
        acc_ref[...] = jnp.zeros_like(acc_ref)

    gates = _peer_gate(r2_ref, e2_ref, l1_ref, c1_ref, n1_per_tile)
    at = _nt_dot(u_ref[...], h2_ref[...])
    parts = []
    for a in range(n1_per_tile):
        x = at[a * PEER_NKEYS:(a + 1) * PEER_NKEYS, :]
        act = x * (1.0 + lax.erf(x * math.sqrt(0.5)))
        parts.append(gates[a] * act.astype(BF16))
    ga = jnp.concatenate(parts, axis=0)
    acc_ref[...] += jnp.dot(vt_ref[...], ga, preferred_element_type=F32)

    @pl.when(j == pl.num_programs(1) - 1)
    def _():
        x2 = x1_ref[...] + acc_ref[...].T
        o_ref[...] = x2 * lax.rsqrt(jnp.mean(x2 * x2, axis=-1, keepdims=True) + NORM_EPS) * gf_ref[...]


def _peer_dense(h2, u_b, vt_b, r2, e2, l1, c1, x1, gf, tt, w):
    t, d = h2.shape
    ne = u_b.shape[0]
    n1_per_tile = w // PEER_NKEYS
    key_spec = pl.BlockSpec((PEER_HEADS, PEER_NKEYS, tt), lambda i, j: (0, 0, i))
    n1_spec = pl.BlockSpec((PEER_HEADS, n1_per_tile, tt), lambda i, j: (0, j, i))
    return pl.pallas_call(
        functools.partial(_peer_dense_kernel, n1_per_tile=n1_per_tile),
        grid=(t // tt, ne // w),
        in_specs=[pl.BlockSpec((tt, d), lambda i, j: (i, 0)),
                  pl.BlockSpec((w, d), lambda i, j: (j, 0)),
                  pl.BlockSpec((d, w), lambda i, j: (0, j)),
                  key_spec, key_spec, n1_spec, n1_spec,
                  pl.BlockSpec((tt, d), lambda i, j: (i, 0)),
                  pl.BlockSpec((1, d), lambda i, j: (0, 0))],
        out_specs=pl.BlockSpec((tt, d), lambda i, j: (i, 0)),
        out_shape=jax.ShapeDtypeStruct((t, d), F32),
        scratch_shapes=[pltpu.VMEM((d, tt), F32)],
        compiler_params=_params(("parallel", "arbitrary")),
    )(h2, u_b, vt_b, r2, e2, l1, c1, x1, gf)


def _alibi_slopes(n_heads):
    return jnp.asarray([2.0 ** (-8.0 * (i + 1) / n_heads) for i in range(n_heads)], dtype=F32)


def _layer(x2, b, s, g1, w_in, conv_w, conv_b, ln_g, ln_b, w_pw, b_pw, w_ao, w_o, g2, w_q, keys1, keys2, u_tab, v_tab, gf):
    t, d = x2.shape
    row = lambda a: a.reshape(1, -1).astype(F32)
    u, q, k, vt, gc, ga, km = _inproj(x2, row(g1), w_in.astype(BF16), tm=TOK_TILE, seq=s)
    cm = _conv(u.reshape(b, s, -1), gc.reshape(b, s, d), conv_w, row(conv_b), row(ln_g), row(ln_b),
               w_pw.astype(BF16), row(b_pw), ts=TOK_TILE)
    at = _attn(q.reshape(b, s, ATTN_W), k.reshape(b, s, ATTN_W), vt,
               km.reshape(b, s // MOBA_BLOCK, ATTN_W), _alibi_slopes(ATTN_HEADS), heads=ATTN_HEADS_PER_STEP)
    x1, h2, qp = _merge(at.reshape(t, ATTN_W), cm.reshape(t, d), ga, x2, w_ao.astype(BF16), w_o.astype(BF16),
                        row(g2), w_q.astype(BF16), tm=TOK_TILE)
    r2, e2, l1, c1 = _peer_topk(qp, keys1, keys2, tt=TOPK_TOK_TILE)
    return _peer_dense(h2, u_tab.astype(BF16), v_tab.astype(BF16).T, r2, e2, l1, c1, x1, gf,
                       tt=PEER_TOK_TILE, w=PEER_EXP_TILE)


def kernel(x, g_norm1, w_in, conv_w, conv_b, conv_ln_g, conv_ln_b, w_conv_pw, b_conv_pw, w_attn_out, w_out,
           g_norm2, w_peer_q, peer_keys1, peer_keys2, peer_u, peer_v, g_final):
    b, s, d = x.shape
    depth = w_in.shape[0]
    assert depth == 1, "the final RMSNorm is fused into the last layer's PEER kernel"
    assert s % TOK_TILE == 0 and d % LANES == 0
    x2 = x.reshape(b * s, d)
    l = 0
    out = _layer(x2, b, s, g_norm1[l], w_in[l], conv_w[l], conv_b[l], conv_ln_g[l], conv_ln_b[l], w_conv_pw[l],
                 b_conv_pw[l], w_attn_out[l], w_out[l], g_norm2[l], w_peer_q[l], peer_keys1[l], peer_keys2[l],
                 peer_u[l], peer_v[l], g_final.reshape(1, -1).astype(F32))
    return out.reshape(b, s, d)
```
